```python
import jax, jax.numpy as jnp
from jax import lax
import numpy as np

D_MODEL = 1024
BATCH = 1
SEQ = 16384
DEPTH = 4
DEC_BATCH = 8
DEC_SEQ = 32
PAST_LEN = 2048

CHUNK = 64
N_MIXERS = 2
N_CONV = (DEPTH + 1) // 2
N_FOX = DEPTH // 2
CONV_WIDTH = 3
N_HEADS = 16
HEAD_DIM = D_MODEL // N_HEADS
D_FF = 4 * D_MODEL
Q_BLOCK = 128
EPS = 1e-5
NEG = -1e30
FORGET_BIAS_CENTER = 2.0

kernel_name = "chunk_stream_conv_fox_hybrid"


def rmsnorm(x, g):
    xf = x.astype(jnp.float32)
    r = lax.rsqrt(jnp.mean(xf * xf, axis=-1, keepdims=True) + EPS)
    return (xf * r * g.astype(jnp.float32)).astype(x.dtype)


def conv_mixer(x, past, w_in, w_conv, w_out):
    T = x.shape[1]
    bch = x @ w_in
    b, c, h = jnp.split(bch, 3, axis=-1)
    u = c * h
    up = jnp.concatenate([past.astype(u.dtype), u], axis=1)
    conv = sum(w_conv[j] * up[:, j:j + T] for j in range(CONV_WIDTH))
    y = (b * conv) @ w_out
    return y, up[:, -(CONV_WIDTH - 1):]


def fox_project(x, w_in, b_f):
    N, T, _ = x.shape
    z = x @ w_in
    q = z[..., :D_MODEL].reshape(N, T, N_HEADS, HEAD_DIM) * (HEAD_DIM ** -0.5)
    k = z[..., D_MODEL:2 * D_MODEL].reshape(N, T, N_HEADS, HEAD_DIM)
    v = z[..., 2 * D_MODEL:3 * D_MODEL].reshape(N, T, N_HEADS, HEAD_DIM)
    logf = jax.nn.log_sigmoid((z[..., 3 * D_MODEL:] + b_f).astype(jnp.float32))
    return q, k, v, logf


def fox_prompt_attn(q, k, v, logf):
    B, S, H, Dh = q.shape
    nb = S // Q_BLOCK
    c = jnp.cumsum(logf.astype(jnp.float32), axis=1)
    c_k = c.transpose(0, 2, 1)
    kpos = jnp.arange(S)
    qb = q.reshape(B, nb, Q_BLOCK, H, Dh).transpose(1, 0, 2, 3, 4)
    cb = c.reshape(B, nb, Q_BLOCK, H).transpose(1, 0, 3, 2)

    def block(args):
        qi, ci, i = args
        qpos = i * Q_BLOCK + jnp.arange(Q_BLOCK)
        s = jnp.einsum('bqhd,bkhd->bhqk', qi, k, preferred_element_type=jnp.float32)
        s = s + ci[..., None] - c_k[:, :, None, :]
        s = jnp.where(kpos[None, :] <= qpos[:, None], s, NEG)
        p = jax.nn.softmax(s, axis=-1)
        return jnp.einsum('bhqk,bkhd->bqhd', p.astype(v.dtype), v)

    out = lax.map(block, (qb, cb, jnp.arange(nb)))
    return out.transpose(1, 0, 2, 3, 4).reshape(B, S, H * Dh)


def fox_sample_attn(q, k_all, v_all, logf_all, past_len):
    N, T, H, Dh = q.shape
    L = k_all.shape[1]
    c = jnp.cumsum(logf_all.astype(jnp.float32), axis=1).transpose(0, 2, 1)
    c_q = c[:, :, past_len:]
    s = jnp.einsum('bqhd,bkhd->bhqk', q, k_all, preferred_element_type=jnp.float32)
    s = s + c_q[..., None] - c[:, :, None, :]
    qpos = past_len + jnp.arange(T)
    kpos = jnp.arange(L)
    s = jnp.where(kpos[None, :] <= qpos[:, None], s, NEG)
    p = jax.nn.softmax(s, axis=-1)
    out = jnp.einsum('bhqk,bkhd->bqhd', p.astype(v_all.dtype), v_all)
    return out.reshape(N, T, H * Dh)


def trunk(x, conv_past, fox_past, norm_mix, norm_mlp, norm_final, conv_w_in, conv_w, conv_w_out,
          fox_w_in, fox_b_f, fox_w_out, mlp_w1, mlp_w2):
    conv_states, ks, vs, lfs = [], [], [], []
    for i in range(DEPTH):
        j = i // N_MIXERS
        h = rmsnorm(x, norm_mix[i])
        if i % N_MIXERS == 0:
            y, st = conv_mixer(h, conv_past[j], conv_w_in[j], conv_w[j], conv_w_out[j])
            conv_states.append(st)
        else:
            q, k, v, logf = fox_project(h, fox_w_in[j], fox_b_f[j])
            if fox_past is None:
                o = fox_prompt_attn(q, k, v, logf)
            else:
                ck, cv, cl = fox_past
                k_all = jnp.concatenate([ck[j].astype(k.dtype), k], axis=1)
                v_all = jnp.concatenate([cv[j].astype(v.dtype), v], axis=1)
                l_all = jnp.concatenate([cl[j].astype(jnp.float32), logf], axis=1)
                o = fox_sample_attn(q, k_all, v_all, l_all, ck.shape[2])
            y = o @ fox_w_out[j]
            ks.append(k)
            vs.append(v)
            lfs.append(logf)
        x = x + y
        h = rmsnorm(x, norm_mlp[i])
        x = x + jnp.square(jax.nn.relu(h @ mlp_w1[i])) @ mlp_w2[i]
    x = rmsnorm(x, norm_final)
    return x, jnp.stack(conv_states), jnp.stack(ks), jnp.stack(vs), jnp.stack(lfs)


def setup_inputs(seed: int = 0) -> dict:
    key = jax.random.key(seed)
    ks = jax.random.split(key, 20)
    f32 = jnp.float32
    nrm = lambda k, shape, scale: jax.random.normal(k, shape, f32) * scale
    d = D_MODEL
    return {
        "x_prompt": nrm(ks[0], (BATCH, SEQ, d), 1.0),
        "x_sample": nrm(ks[1], (DEC_BATCH, DEC_SEQ, d), 1.0),
        "state_conv": nrm(ks[2], (N_CONV, DEC_BATCH, CONV_WIDTH - 1, d), 1.0),
        "cache_k": nrm(ks[3], (N_FOX, DEC_BATCH, PAST_LEN, N_HEADS, HEAD_DIM), 1.0),
        "cache_v": nrm(ks[4], (N_FOX, DEC_BATCH, PAST_LEN, N_HEADS, HEAD_DIM), 1.0),
        "cache_logf": jax.nn.log_sigmoid(FORGET_BIAS_CENTER + nrm(ks[5], (N_FOX, DEC_BATCH, PAST_LEN, N_HEADS), 1.0)),
        "norm_mix": 1.0 + nrm(ks[6], (DEPTH, d), 0.02),
        "norm_mlp": 1.0 + nrm(ks[7], (DEPTH, d), 0.02),
        "norm_final": 1.0 + nrm(ks[8], (d,), 0.02),
        "conv_w_in": nrm(ks[9], (N_CONV, d, 3 * d), d ** -0.5),
        "conv_w": nrm(ks[10], (N_CONV, CONV_WIDTH, d), CONV_WIDTH ** -0.5),
        "conv_w_out": nrm(ks[11], (N_CONV, d, d), d ** -0.5),
        "fox_w_in": nrm(ks[12], (N_FOX, d, 3 * d + N_HEADS), d ** -0.5),
        "fox_b_f": FORGET_BIAS_CENTER + nrm(ks[13], (N_FOX, N_HEADS), 0.5),
        "fox_w_out": nrm(ks[14], (N_FOX, d, d), d ** -0.5),
        "mlp_w1": nrm(ks[15], (DEPTH, d, D_FF), d ** -0.5),
        "mlp_w2": nrm(ks[16], (DEPTH, D_FF, d), D_FF ** -0.5),
    }


def reference(x_prompt, x_sample, state_conv, cache_k, cache_v, cache_logf, norm_mix, norm_mlp,
              norm_final, conv_w_in, conv_w, conv_w_out, fox_w_in, fox_b_f, fox_w_out, mlp_w1, mlp_w2):
    weights = (norm_mix, norm_mlp, norm_final, conv_w_in, conv_w, conv_w_out,
               fox_w_in, fox_b_f, fox_w_out, mlp_w1, mlp_w2)
    zero_conv = jnp.zeros((N_CONV, x_prompt.shape[0], CONV_WIDTH - 1, D_MODEL), x_prompt.dtype)
    y_prompt, p_conv, p_k, p_v, p_logf = trunk(x_prompt, zero_conv, None, *weights)
    y_sample, s_conv, s_k, s_v, s_logf = trunk(x_sample, state_conv, (cache_k, cache_v, cache_logf), *weights)
    return (y_prompt, y_sample, p_conv, p_k, p_v, p_logf, s_conv, s_k, s_v, s_logf)
```

```python
import functools

import numpy as np
import jax
import jax.numpy as jnp
from jax import lax
from jax.experimental import pallas as pl
from jax.experimental.pallas import tpu as pltpu

D_MODEL = 1024
N_HEADS = 16
HEAD_DIM = 64
D_FF = 4 * D_MODEL
CONV_WIDTH = 3
EPS = 1e-5
NEG = -1e30

LANES = 128
GROUPS = N_HEADS * LANES
CARRY_ROWS = 8
VMEM_LIMIT = 56 * 1024 * 1024

F32 = jnp.float32
BF16 = jnp.bfloat16


def _rms(x, g):
    r = lax.rsqrt(jnp.mean(x * x, axis=-1, keepdims=True) + EPS)
    return x * r * g


def _log_sigmoid(x):
    return jnp.minimum(x, 0.0) - jnp.log1p(jnp.exp(-jnp.abs(x)))


def _dot(a, b):
    return jnp.dot(a, b, preferred_element_type=F32)


def _const_spec(shape):
    return pl.BlockSpec(shape, lambda *_: (0,) * len(shape), pipeline_mode=pl.Buffered(1))


def _params(n_axes):
    return pltpu.CompilerParams(dimension_semantics=("arbitrary",) * n_axes,
                                vmem_limit_bytes=VMEM_LIMIT)


def _mlp_kernel(x_ref, g_ref, w1_ref, w2_ref, gf_ref, o_ref, *, final_norm, n_chunks):
    x = x_ref[...]
    h = _rms(x, g_ref[...]).astype(BF16)
    ck = D_FF // n_chunks
    acc = x
    for c in range(n_chunks):
        a = _dot(h, w1_ref[:, c * ck:(c + 1) * ck])
        a = jnp.square(jnp.maximum(a, 0.0)).astype(BF16)
        acc = acc + _dot(a, w2_ref[c * ck:(c + 1) * ck, :])
    if final_norm:
        acc = _rms(acc, gf_ref[...])
    o_ref[...] = acc


def _mlp(x, g, w1, w2, gf, *, tm, final_norm):
    rows = x.shape[0]
    return pl.pallas_call(
        functools.partial(_mlp_kernel, final_norm=final_norm, n_chunks=4),
        grid=(rows // tm,),
        in_specs=[pl.BlockSpec((tm, D_MODEL), lambda i: (i, 0)),
                  _const_spec((1, D_MODEL)),
                  _const_spec((D_MODEL, D_FF)),
                  _const_spec((D_FF, D_MODEL)),
                  _const_spec((1, D_MODEL))],
        out_specs=pl.BlockSpec((tm, D_MODEL), lambda i: (i, 0)),
        out_shape=jax.ShapeDtypeStruct((rows, D_MODEL), F32),
        compiler_params=_params(1),
        name="mlp",
    )(x, g, w1, w2, gf)


def _conv_kernel(x_ref, past_ref, g_ref, win_ref, cw_ref, wout_ref, y_ref, st_ref, up_ref,
                 *, n_streams, seg):
    d = D_MODEL
    lo = CARRY_ROWS - (CONV_WIDTH - 1)

    @pl.when(pl.program_id(0) == 0)
    def _():
        up_ref[:, lo:CARRY_ROWS, :] = past_ref[...]

    x = x_ref[...]
    h = _rms(x, g_ref[...]).astype(BF16)
    b = _dot(h, win_ref[:, :d])
    u = _dot(h, win_ref[:, d:2 * d]) * _dot(h, win_ref[:, 2 * d:])
    u3 = u.reshape(n_streams, seg, d)
    up_ref[:, CARRY_ROWS:CARRY_ROWS + seg, :] = u3
    conv = cw_ref[2:3, :].reshape(1, 1, d) * u3
    for j in range(CONV_WIDTH - 1):
        conv = conv + cw_ref[j:j + 1, :].reshape(1, 1, d) * up_ref[:, lo + j:lo + j + seg, :]
    gated = (b * conv.reshape(n_streams * seg, d)).astype(BF16)
    y_ref[...] = x + _dot(gated, wout_ref[...])
    state = up_ref[:, CARRY_ROWS + seg - (CONV_WIDTH - 1):CARRY_ROWS + seg, :]
    st_ref[...] = state
    up_ref[:, lo:CARRY_ROWS, :] = state


def _conv_mixer(x, past, g, w_in, cw, w_out, *, n_streams, seg):
    rows = x.shape[0]
    tm = n_streams * seg
    assert rows % tm == 0 and (rows == tm or n_streams == 1)
    state_shape = (n_streams, CONV_WIDTH - 1, D_MODEL)
    return pl.pallas_call(
        functools.partial(_conv_kernel, n_streams=n_streams, seg=seg),
        grid=(rows // tm,),
        in_specs=[pl.BlockSpec((tm, D_MODEL), lambda i: (i, 0)),
                  _const_spec(state_shape),
                  _const_spec((1, D_MODEL)),
                  _const_spec((D_MODEL, 3 * D_MODEL)),
                  _const_spec((CONV_WIDTH, D_MODEL)),
                  _const_spec((D_MODEL, D_MODEL))],
        out_specs=[pl.BlockSpec((tm, D_MODEL), lambda i: (i, 0)),
                   pl.BlockSpec(state_shape, lambda i: (0, 0, 0))],
        out_shape=[jax.ShapeDtypeStruct((rows, D_MODEL), F32),
                   jax.ShapeDtypeStruct(state_shape, F32)],
        scratch_shapes=[pltpu.VMEM((n_streams, CARRY_ROWS + seg, D_MODEL), F32)],
        compiler_params=_params(1),
        name="conv_mixer",
    )(x, past, g, w_in, cw, w_out)


def _segment_cumsum(x, seg):
    t = lax.broadcasted_iota(jnp.int32, x.shape, 0) & (seg - 1)
    step = 1
    while step < seg:
        x = x + jnp.where(t >= step, pltpu.roll(x, step, axis=0), 0.0)
        step *= 2
    return x


def _split3(c):
    c1 = c.astype(BF16).astype(F32)
    r1 = c - c1
    c2 = r1.astype(BF16).astype(F32)
    c3 = (r1 - c2).astype(BF16).astype(F32)
    lane = lax.broadcasted_iota(jnp.int32, c.shape, 1)
    packed = (c1 + pltpu.roll(c2, N_HEADS, axis=1) + pltpu.roll(c3, 2 * N_HEADS, axis=1)
              + jnp.where(lane == 3 * N_HEADS, 1.0, 0.0))
    return packed.astype(BF16)


def _placement_matrix():
    p = np.zeros((LANES, 2 * GROUPS), np.float32)
    one = 3 * N_HEADS
    for h in range(N_HEADS):
        qb = h * LANES + HEAD_DIM
        kb = GROUPS + h * LANES + HEAD_DIM
        for s in range(3):
            p[s * N_HEADS + h, qb + s] = 1.0
            p[one, qb + 3 + s] = 1.0
            p[one, kb + s] = 1.0
            p[s * N_HEADS + h, kb + 3 + s] = -1.0
    return p


def _store_groups(z, extra, ref):
    rows = z.shape[0]
    lane = lax.broadcasted_iota(jnp.int32, (rows, LANES), 1)
    low = lane < HEAD_DIM
    ones_col = jnp.where(lane == HEAD_DIM, 1.0, 0.0)
    for p in range(N_HEADS // 2):
        zc = z[:, p * LANES:(p + 1) * LANES]
        halves = (zc, pltpu.roll(zc, HEAD_DIM, axis=1))
        for e in range(2):
            h = 2 * p + e
            fill = ones_col if extra is None else extra[:, h * LANES:(h + 1) * LANES]
            ref[:, h * LANES:(h + 1) * LANES] = jnp.where(low, halves[e], fill).astype(BF16)


def _fox_proj_kernel(x_ref, g_ref, wqkv_ref, wf_ref, bf_ref, pmat_ref,
                     k_ref, v_ref, lf_ref, qa_ref, ka_ref, va_ref, carry_ref, *, seg, carried):
    d = D_MODEL
    rows = x_ref.shape[0]

    @pl.when(pl.program_id(0) == 0)
    def _():
        carry_ref[...] = jnp.zeros_like(carry_ref)

    h = _rms(x_ref[...], g_ref[...]).astype(BF16)
    q = _dot(h, wqkv_ref[:, :d]) * (HEAD_DIM ** -0.5)
    k = _dot(h, wqkv_ref[:, d:2 * d])
    v = _dot(h, wqkv_ref[:, 2 * d:])
    zf = _dot(h, wf_ref[...]) + bf_ref[...]
    lane = lax.broadcasted_iota(jnp.int32, (rows, LANES), 1)
    lf = jnp.where(lane < N_HEADS, _log_sigmoid(zf), 0.0)
    k_ref[...] = k
    v_ref[...] = v
    lf_ref[...] = lf[:, :N_HEADS]

    c = _segment_cumsum(lf, seg)
    if carried:
        c = c + carry_ref[0:1, :]
        carry_ref[0:1, :] = c[rows - 1:rows, :]
    aug = _dot(_split3(c), pmat_ref[...])
    _store_groups(q, aug[:, :GROUPS], qa_ref)
    _store_groups(k, aug[:, GROUPS:], ka_ref)
    _store_groups(v, None, va_ref)


def _fox_proj(x, g, wqkv, wf, bf, pmat, *, tm, seg, carried):
    rows = x.shape[0]
    assert rows % tm == 0 and tm % seg == 0 and (not carried or seg == tm)
    row_spec = lambda w: pl.BlockSpec((tm, w), lambda i: (i, 0))
    return pl.pallas_call(
        functools.partial(_fox_proj_kernel, seg=seg, carried=carried),
        grid=(rows // tm,),
        in_specs=[row_spec(D_MODEL),
                  _const_spec((1, D_MODEL)),
                  _const_spec((D_MODEL, 3 * D_MODEL)),
                  _const_spec((D_MODEL, LANES)),
                  _const_spec((1, LANES)),
                  _const_spec((LANES, 2 * GROUPS))],
        out_specs=[row_spec(D_MODEL), row_spec(D_MODEL), row_spec(N_HEADS),
                   row_spec(GROUPS), row_spec(GROUPS), row_spec(GROUPS)],
        out_shape=[jax.ShapeDtypeStruct((rows, D_MODEL), F32),
                   jax.ShapeDtypeStruct((rows, D_MODEL), F32),
                   jax.ShapeDtypeStruct((rows, N_HEADS), F32),
                   jax.ShapeDtypeStruct((rows, GROUPS), BF16),
                   jax.ShapeDtypeStruct((rows, GROUPS), BF16),
                   jax.ShapeDtypeStruct((rows, GROUPS), BF16)],
        scratch_shapes=[pltpu.VMEM((CARRY_ROWS, LANES), F32)],
        compiler_params=_params(1),
        name="fox_proj",
    )(x, g, wqkv, wf, bf, pmat)


def _cache_prep_kernel(ck_ref, cv_ref, cl_ref, pmat_ref, ka_ref, va_ref, c_ref, *, past_len, tr):
    r = pl.program_id(1)

    @pl.when(r == 0)
    def _():
        c = _segment_cumsum(cl_ref[...], past_len)
        c_ref[...] = c - c[past_len - 1:past_len, :]

    c = c_ref[pl.ds(pl.multiple_of(r * tr, tr), tr), :]
    aug = _dot(_split3(c), pmat_ref[...])
    _store_groups(ck_ref[...], aug, ka_ref)
    _store_groups(cv_ref[...], None, va_ref)


def _cache_prep(ck, cv, cl, pmat_k, *, n_streams, past_len, tr):
    nr = past_len // tr
    row_spec = lambda w: pl.BlockSpec((tr, w), lambda n, r: (n * nr + r, 0))
    return pl.pallas_call(
        functools.partial(_cache_prep_kernel, past_len=past_len, tr=tr),
        grid=(n_streams, nr),
        in_specs=[row_spec(D_MODEL), row_spec(D_MODEL),
                  pl.BlockSpec((past_len, LANES), lambda n, r: (n, 0)),
                  _const_spec((LANES, GROUPS))],
        out_specs=[row_spec(GROUPS), row_spec(GROUPS)],
        out_shape=[jax.ShapeDtypeStruct((n_streams * past_len, GROUPS), BF16)] * 2,
        scratch_shapes=[pltpu.VMEM((past_len, LANES), F32)],
        compiler_params=_params(2),
        name="cache_prep",
    )(ck, cv, cl, pmat_k)


def _attend(q, k, v, m, acc, mask=None):
    s = lax.dot_general(q, k, (((1,), (1,)), ((), ())), preferred_element_type=F32)
    if mask is not None:
        s = jnp.where(mask, s, NEG)
    m_new = jnp.maximum(m, jnp.max(s, axis=-1, keepdims=True))
    p = jnp.exp(s - m_new).astype(BF16)
    acc = jnp.exp(m - m_new) * acc + _dot(p, v)
    return m_new, acc


def _causal_mask(n):
    return (lax.broadcasted_iota(jnp.int32, (n, n), 1)
            <= lax.broadcasted_iota(jnp.int32, (n, n), 0))


def _finish(accs):
    outs = [a[:, :HEAD_DIM] / a[:, HEAD_DIM:HEAD_DIM + 1] for a in accs]
    return jnp.concatenate(outs, axis=-1)


def _prompt_attn_kernel(qa_ref, ka_ref, va_ref, o_ref, *, tq):
    i = pl.program_id(1)
    mask = _causal_mask(tq)
    diag = pl.ds(pl.multiple_of(i * tq, tq), tq)
    accs = []
    for e in range(2):
        sl = slice(e * LANES, (e + 1) * LANES)
        q = qa_ref[:, sl]
        m0 = jnp.full((tq, 1), NEG, F32)
        acc0 = jnp.zeros((tq, LANES), F32)
        carry = _attend(q, ka_ref[diag, sl], va_ref[diag, sl], m0, acc0, mask)

        def body(jj, carry, q=q, sl=sl):
            rows = pl.ds(pl.multiple_of((i - 1 - jj) * tq, tq), tq)
            return _attend(q, ka_ref[rows, sl], va_ref[rows, sl], *carry)

        _, acc = lax.fori_loop(0, i, body, carry)
        accs.append(acc)
    o_ref[...] = _finish(accs)


def _prompt_attn(qa, ka, va, *, tq):
    rows = qa.shape[0]
    pair = 2 * LANES
    return pl.pallas_call(
        functools.partial(_prompt_attn_kernel, tq=tq),
        grid=(N_HEADS // 2, rows // tq),
        in_specs=[pl.BlockSpec((tq, pair), lambda hp, i: (i, hp)),
                  pl.BlockSpec((rows, pair), lambda hp, i: (0, hp)),
                  pl.BlockSpec((rows, pair), lambda hp, i: (0, hp))],
        out_specs=pl.BlockSpec((tq, LANES), lambda hp, i: (i, hp)),
        out_shape=jax.ShapeDtypeStruct((rows, D_MODEL), F32),
        compiler_params=_params(2),
        name="prompt_attn",
    )(qa, ka, va)


def _sample_attn_kernel(qa_ref, kn_ref, vn_ref, kc_ref, vc_ref, o_ref):
    t = qa_ref.shape[0]
    mask = _causal_mask(t)
    accs = []
    for e in range(2):
        sl = slice(e * LANES, (e + 1) * LANES)
        q = qa_ref[:, sl]
        m0 = jnp.full((t, 1), NEG, F32)
        acc0 = jnp.zeros((t, LANES), F32)
        m, acc = _attend(q, kn_ref[:, sl], vn_ref[:, sl], m0, acc0, mask)
        _, acc = _attend(q, kc_ref[:, sl], vc_ref[:, sl], m, acc)
        accs.append(acc)
    o_ref[...] = _finish(accs)


def _sample_attn(qa, kn, vn, kc, vc, *, n_streams, seg, past_len):
    pair = 2 * LANES
    new_spec = pl.BlockSpec((seg, pair), lambda n, hp: (n, hp))
    old_spec = pl.BlockSpec((past_len, pair), lambda n, hp: (n, hp))
    return pl.pallas_call(
        _sample_attn_kernel,
        grid=(n_streams, N_HEADS // 2),
        in_specs=[new_spec, new_spec, new_spec, old_spec, old_spec],
        out_specs=pl.BlockSpec((seg, LANES), lambda n, hp: (n, hp)),
        out_shape=jax.ShapeDtypeStruct((n_streams * seg, D_MODEL), F32),
        compiler_params=_params(2),
        name="sample_attn",
    )(qa, kn, vn, kc, vc)


def _out_proj_kernel(x_ref, o_ref, w_ref, y_ref):
    y_ref[...] = x_ref[...] + _dot(o_ref[...].astype(BF16), w_ref[...])


def _out_proj(x, o, w, *, tm):
    rows = x.shape[0]
    spec = pl.BlockSpec((tm, D_MODEL), lambda i: (i, 0))
    return pl.pallas_call(
        _out_proj_kernel,
        grid=(rows // tm,),
        in_specs=[spec, spec, _const_spec((D_MODEL, D_MODEL))],
        out_specs=spec,
        out_shape=jax.ShapeDtypeStruct((rows, D_MODEL), F32),
        compiler_params=_params(1),
        name="out_proj",
    )(x, o, w)


def _trunk(x, conv_past, cache, w, *, n_streams, seg, tm, tq):
    depth = w["mlp_w1"].shape[0]
    conv_states, ks, vs, lfs = [], [], [], []
    carried = n_streams == 1
    for i in range(depth):
        j = i // 2
        g_mix = w["norm_mix"][i:i + 1]
        if i % 2 == 0:
            x, st = _conv_mixer(x, conv_past[j], g_mix, w["conv_w_in"][j], w["conv_w"][j],
                                w["conv_w_out"][j], n_streams=n_streams,
                                seg=tm if carried else seg)
            conv_states.append(st)
        else:
            k, v, lf, qa, ka, va = _fox_proj(
                x, g_mix, w["fox_wqkv"][j], w["fox_wf"][j], w["fox_bf"][j], w["pmat"],
                tm=min(tm, 256), seg=min(tm, 256) if carried else seg, carried=carried)
            if cache is None:
                o = _prompt_attn(qa, ka, va, tq=tq)
            else:
                ck, cv, cl = cache
                past_len = ck.shape[2]
                flat = n_streams * past_len
                kc, vc = _cache_prep(ck[j].reshape(flat, D_MODEL), cv[j].reshape(flat, D_MODEL),
                                     cl[j], w["pmat"][:, GROUPS:], n_streams=n_streams,
                                     past_len=past_len, tr=512)
                o = _sample_attn(qa, ka, va, kc, vc, n_streams=n_streams, seg=seg,
                                 past_len=past_len)
            x = _out_proj(x, o, w["fox_w_out"][j], tm=tm)
            ks.append(k)
            vs.append(v)
            lfs.append(lf)
        x = _mlp(x, w["norm_mlp"][i:i + 1], w["mlp_w1"][i], w["mlp_w2"][i], w["norm_final"],
                 tm=tm, final_norm=i == depth - 1)
    return x, jnp.stack(conv_states), jnp.stack(ks), jnp.stack(vs), jnp.stack(lfs)


def kernel(x_prompt, x_sample, state_conv, cache_k, cache_v, cache_logf, norm_mix, norm_mlp,
           norm_final, conv_w_in, conv_w, conv_w_out, fox_w_in, fox_b_f, fox_w_out, mlp_w1, mlp_w2):
    d = D_MODEL
    n_fox = fox_w_in.shape[0]
    batch, seq, _ = x_prompt.shape
    dec_batch, dec_seq, _ = x_sample.shape
    past_len = cache_k.shape[2]
    assert batch == 1

    lane_pad = ((0, 0), (0, 0), (0, LANES - N_HEADS))
    w = dict(
        norm_mix=norm_mix, norm_mlp=norm_mlp, norm_final=norm_final.reshape(1, d),
        conv_w_in=conv_w_in.astype(BF16), conv_w=conv_w, conv_w_out=conv_w_out.astype(BF16),
        fox_wqkv=fox_w_in[:, :, :3 * d].astype(BF16),
        fox_wf=jnp.pad(fox_w_in[:, :, 3 * d:], lane_pad).astype(BF16),
        fox_bf=jnp.pad(fox_b_f.reshape(n_fox, 1, N_HEADS), lane_pad),
        fox_w_out=fox_w_out.astype(BF16),
        mlp_w1=mlp_w1.astype(BF16), mlp_w2=mlp_w2.astype(BF16),
        pmat=jnp.asarray(_placement_matrix(), BF16),
    )

    zero_conv = jnp.zeros((conv_w.shape[0], batch, CONV_WIDTH - 1, d), F32)
    y_p, p_conv, p_k, p_v, p_lf = _trunk(
        x_prompt.reshape(seq, d), zero_conv, None, w, n_streams=1, seg=seq, tm=512, tq=512)

    cl = jnp.pad(cache_logf.reshape(n_fox, dec_batch * past_len, N_HEADS), lane_pad)
    y_s, s_conv, s_k, s_v, s_lf = _trunk(
        x_sample.reshape(dec_batch * dec_seq, d), state_conv, (cache_k, cache_v, cl), w,
        n_streams=dec_batch, seg=dec_seq, tm=dec_batch * dec_seq, tq=None)

    heads = (N_HEADS, HEAD_DIM)
    return (y_p.reshape(batch, seq, d), y_s.reshape(dec_batch, dec_seq, d),
            p_conv,
            p_k.reshape(n_fox, batch, seq, *heads), p_v.reshape(n_fox, batch, seq, *heads),
            p_lf.reshape(n_fox, batch, seq, N_HEADS),
            s_conv,
            s_k.reshape(n_fox, dec_batch, dec_seq, *heads),
            s_v.reshape(n_fox, dec_batch, dec_seq, *heads),
            s_lf.reshape(n_fox, dec_batch, dec_seq, N_HEADS))
```

```python
import functools

import numpy as np
import jax
import jax.numpy as jnp
from jax import lax
from jax.experimental import pallas as pl
from jax.experimental.pallas import tpu as pltpu

D_MODEL = 1024
N_HEADS = 16
HEAD_DIM = 64
D_FF = 4 * D_MODEL
CONV_WIDTH = 3
EPS = 1e-5
NEG = -1e30

LANES = 128
GROUPS = N_HEADS * LANES
CARRY_ROWS = 8
VMEM_LIMIT = 56 * 1024 * 1024

F32 = jnp.float32
BF16 = jnp.bfloat16

LOG2E = 1.4426950408889634
Q_SCALE = HEAD_DIM ** -0.5 * LOG2E
BIG = 3.0e38
ZERO_WEIGHT_GAP = 150.0
NORM_SLACK = 1.02
BOUND_SLACK = 1.0


def _rms(x, g):
    r = lax.rsqrt(jnp.mean(x * x, axis=-1, keepdims=True) + EPS)
    return x * r * g


def _log_sigmoid(x):
    return jnp.minimum(x, 0.0) - jnp.log1p(jnp.exp(-jnp.abs(x)))


def _dot(a, b):
    return jnp.dot(a, b, preferred_element_type=F32)


def _const_spec(shape):
    return pl.BlockSpec(shape, lambda *_: (0,) * len(shape), pipeline_mode=pl.Buffered(1))


def _params(n_axes):
    return pltpu.CompilerParams(dimension_semantics=("arbitrary",) * n_axes,
                                vmem_limit_bytes=VMEM_LIMIT)


def _mlp_kernel(x_ref, g_ref, w1_ref, w2_ref, gf_ref, o_ref, *, final_norm, n_chunks):
    x = x_ref[...]
    h = _rms(x, g_ref[...]).astype(BF16)
    ck = D_FF // n_chunks
    acc = x
    for c in range(n_chunks):
        a = _dot(h, w1_ref[:, c * ck:(c + 1) * ck])
        a = jnp.square(jnp.maximum(a, 0.0)).astype(BF16)
        acc = acc + _dot(a, w2_ref[c * ck:(c + 1) * ck, :])
    if final_norm:
        acc = _rms(acc, gf_ref[...])
    o_ref[...] = acc


def _mlp(x, g, w1, w2, gf, *, tm, final_norm):
    rows = x.shape[0]
    return pl.pallas_call(
        functools.partial(_mlp_kernel, final_norm=final_norm, n_chunks=4),
        grid=(rows // tm,),
        in_specs=[pl.BlockSpec((tm, D_MODEL), lambda i: (i, 0)),
                  _const_spec((1, D_MODEL)),
                  _const_spec((D_MODEL, D_FF)),
                  _const_spec((D_FF, D_MODEL)),
                  _const_spec((1, D_MODEL))],
        out_specs=pl.BlockSpec((tm, D_MODEL), lambda i: (i, 0)),
        out_shape=jax.ShapeDtypeStruct((rows, D_MODEL), F32),
        compiler_params=_params(1),
        name="mlp",
    )(x, g, w1, w2, gf)


def _conv_kernel(x_ref, past_ref, g_ref, win_ref, cw_ref, wout_ref, y_ref, st_ref, up_ref,
                 *, n_streams, seg):
    d = D_MODEL
    lo = CARRY_ROWS - (CONV_WIDTH - 1)

    @pl.when(pl.program_id(0) == 0)
    def _():
        up_ref[:, lo:CARRY_ROWS, :] = past_ref[...]

    x = x_ref[...]
    h = _rms(x, g_ref[...]).astype(BF16)
    b = _dot(h, win_ref[:, :d])
    u = _dot(h, win_ref[:, d:2 * d]) * _dot(h, win_ref[:, 2 * d:])
    u3 = u.reshape(n_streams, seg, d)
    up_ref[:, CARRY_ROWS:CARRY_ROWS + seg, :] = u3
    conv = cw_ref[2:3, :].reshape(1, 1, d) * u3
    for j in range(CONV_WIDTH - 1):
        conv = conv + cw_ref[j:j + 1, :].reshape(1, 1, d) * up_ref[:, lo + j:lo + j + seg, :]
    gated = (b * conv.reshape(n_streams * seg, d)).astype(BF16)
    y_ref[...] = x + _dot(gated, wout_ref[...])
    state = up_ref[:, CARRY_ROWS + seg - (CONV_WIDTH - 1):CARRY_ROWS + seg, :]
    st_ref[...] = state
    up_ref[:, lo:CARRY_ROWS, :] = state


def _conv_mixer(x, past, g, w_in, cw, w_out, *, n_streams, seg):
    rows = x.shape[0]
    tm = n_streams * seg
    assert rows % tm == 0 and (rows == tm or n_streams == 1)
    state_shape = (n_streams, CONV_WIDTH - 1, D_MODEL)
    return pl.pallas_call(
        functools.partial(_conv_kernel, n_streams=n_streams, seg=seg),
        grid=(rows // tm,),
        in_specs=[pl.BlockSpec((tm, D_MODEL), lambda i: (i, 0)),
                  _const_spec(state_shape),
                  _const_spec((1, D_MODEL)),
                  _const_spec((D_MODEL, 3 * D_MODEL)),
                  _const_spec((CONV_WIDTH, D_MODEL)),
                  _const_spec((D_MODEL, D_MODEL))],
        out_specs=[pl.BlockSpec((tm, D_MODEL), lambda i: (i, 0)),
                   pl.BlockSpec(state_shape, lambda i: (0, 0, 0))],
        out_shape=[jax.ShapeDtypeStruct((rows, D_MODEL), F32),
                   jax.ShapeDtypeStruct(state_shape, F32)],
        scratch_shapes=[pltpu.VMEM((n_streams, CARRY_ROWS + seg, D_MODEL), F32)],
        compiler_params=_params(1),
        name="conv_mixer",
    )(x, past, g, w_in, cw, w_out)


def _segment_cumsum(x, seg):
    t = lax.broadcasted_iota(jnp.int32, x.shape, 0) & (seg - 1)
    step = 1
    while step < seg:
        x = x + jnp.where(t >= step, pltpu.roll(x, step, axis=0), 0.0)
        step *= 2
    return x


def _split3(c):
    c1 = c.astype(BF16).astype(F32)
    r1 = c - c1
    c2 = r1.astype(BF16).astype(F32)
    c3 = (r1 - c2).astype(BF16).astype(F32)
    lane = lax.broadcasted_iota(jnp.int32, c.shape, 1)
    packed = (c1 + pltpu.roll(c2, N_HEADS, axis=1) + pltpu.roll(c3, 2 * N_HEADS, axis=1)
              + jnp.where(lane == 3 * N_HEADS, 1.0, 0.0))
    return packed.astype(BF16)


def _placement_matrix():
    p = np.zeros((LANES, 2 * GROUPS), np.float32)
    one = 3 * N_HEADS
    for h in range(N_HEADS):
        qb = h * LANES + HEAD_DIM
        kb = GROUPS + h * LANES + HEAD_DIM
        for s in range(3):
            p[s * N_HEADS + h, qb + s] = 1.0
            p[one, qb + 3 + s] = 1.0
            p[one, kb + s] = 1.0
            p[s * N_HEADS + h, kb + 3 + s] = -1.0
    return p


def _store_groups(z, extra, ref):
    rows = z.shape[0]
    lane = lax.broadcasted_iota(jnp.int32, (rows, LANES), 1)
    low = lane < HEAD_DIM
    ones_col = jnp.where(lane == HEAD_DIM, 1.0, 0.0)
    for p in range(N_HEADS // 2):
        zc = z[:, p * LANES:(p + 1) * LANES]
        halves = (zc, pltpu.roll(zc, HEAD_DIM, axis=1))
        for e in range(2):
            h = 2 * p + e
            fill = ones_col if extra is None else extra[:, h * LANES:(h + 1) * LANES]
            ref[:, h * LANES:(h + 1) * LANES] = jnp.where(low, halves[e], fill).astype(BF16)


def _head_norms(z, gmat):
    zb = z.astype(BF16).astype(F32)
    return jnp.sqrt(_dot((zb * zb).astype(BF16), gmat))


def _fox_proj_kernel(x_ref, g_ref, wqkv_ref, wf_ref, bf_ref, pmat_ref, gmat_ref,
                     k_ref, v_ref, lf_ref, qa_ref, ka_ref, va_ref, st_ref, carry_ref,
                     *, seg, carried):
    d = D_MODEL
    rows = x_ref.shape[0]

    @pl.when(pl.program_id(0) == 0)
    def _():
        carry_ref[...] = jnp.zeros_like(carry_ref)
        carry_ref[2:3, :] = jnp.full((1, LANES), BIG, F32)

    h = _rms(x_ref[...], g_ref[...]).astype(BF16)
    q = _dot(h, wqkv_ref[:, :d]) * Q_SCALE
    k = _dot(h, wqkv_ref[:, d:2 * d])
    v = _dot(h, wqkv_ref[:, 2 * d:])
    zf = _dot(h, wf_ref[...]) + bf_ref[...]
    lane = lax.broadcasted_iota(jnp.int32, (rows, LANES), 1)
    lf = jnp.where(lane < N_HEADS, _log_sigmoid(zf), 0.0)
    k_ref[...] = k
    v_ref[...] = v
    lf_ref[...] = lf[:, :N_HEADS]

    c = _segment_cumsum(lf, seg)
    if carried:
        c = c + carry_ref[0:1, :]
        carry_ref[0:1, :] = c[rows - 1:rows, :]
    c = c * LOG2E
    aug = _dot(_split3(c), pmat_ref[...])
    _store_groups(q, aug[:, :GROUPS], qa_ref)
    _store_groups(k, aug[:, GROUPS:], ka_ref)
    _store_groups(v, None, va_ref)

    k_max = jnp.max(_head_norms(k, gmat_ref[...]), axis=0, keepdims=True)
    k_pre = jnp.maximum(carry_ref[1:2, :], k_max)
    c_pre = jnp.minimum(carry_ref[2:3, :], jnp.min(c, axis=0, keepdims=True))
    carry_ref[1:2, :] = k_pre
    carry_ref[2:3, :] = c_pre
    st_ref[0, 0:1, :] = jnp.max(_head_norms(q, gmat_ref[...]), axis=0, keepdims=True)
    st_ref[0, 1:2, :] = k_max
    st_ref[0, 2:3, :] = jnp.max(c, axis=0, keepdims=True)
    st_ref[0, 3:4, :] = k_pre
    st_ref[0, 4:5, :] = c_pre
    st_ref[0, 5:8, :] = jnp.zeros((3, LANES), F32)


def _fox_proj(x, g, wqkv, wf, bf, pmat, gmat, *, tm, seg, carried):
    rows = x.shape[0]
    assert rows % tm == 0 and tm % seg == 0 and (not carried or seg == tm)
    row_spec = lambda w: pl.BlockSpec((tm, w), lambda i: (i, 0))
    return pl.pallas_call(
        functools.partial(_fox_proj_kernel, seg=seg, carried=carried),
        grid=(rows // tm,),
        in_specs=[row_spec(D_MODEL),
                  _const_spec((1, D_MODEL)),
                  _const_spec((D_MODEL, 3 * D_MODEL)),
                  _const_spec((D_MODEL, LANES)),
                  _const_spec((1, LANES)),
                  _const_spec((LANES, 2 * GROUPS)),
                  _const_spec((D_MODEL, LANES))],
        out_specs=[row_spec(D_MODEL), row_spec(D_MODEL), row_spec(N_HEADS),
                   row_spec(GROUPS), row_spec(GROUPS), row_spec(GROUPS),
                   pl.BlockSpec((1, CARRY_ROWS, LANES), lambda i: (i, 0, 0))],
        out_shape=[jax.ShapeDtypeStruct((rows, D_MODEL), F32),
                   jax.ShapeDtypeStruct((rows, D_MODEL), F32),
                   jax.ShapeDtypeStruct((rows, N_HEADS), F32),
                   jax.ShapeDtypeStruct((rows, GROUPS), BF16),
                   jax.ShapeDtypeStruct((rows, GROUPS), BF16),
                   jax.ShapeDtypeStruct((rows, GROUPS), BF16),
                   jax.ShapeDtypeStruct((rows // tm, CARRY_ROWS, LANES), F32)],
        scratch_shapes=[pltpu.VMEM((CARRY_ROWS, LANES), F32)],
        compiler_params=_params(1),
        name="fox_proj",
    )(x, g, wqkv, wf, bf, pmat, gmat)


def _cache_prep_kernel(ck_ref, cv_ref, cl_ref, pmat_ref, ka_ref, va_ref, c_ref, *, past_len, tr):
    r = pl.program_id(1)

    @pl.when(r == 0)
    def _():
        c = _segment_cumsum(cl_ref[...], past_len)
        c_ref[...] = (c - c[past_len - 1:past_len, :]) * LOG2E

    c = c_ref[pl.ds(pl.multiple_of(r * tr, tr), tr), :]
    aug = _dot(_split3(c), pmat_ref[...])
    _store_groups(ck_ref[...], aug, ka_ref)
    _store_groups(cv_ref[...], None, va_ref)


def _cache_prep(ck, cv, cl, pmat_k, *, n_streams, past_len, tr):
    nr = past_len // tr
    row_spec = lambda w: pl.BlockSpec((tr, w), lambda n, r: (n * nr + r, 0))
    return pl.pallas_call(
        functools.partial(_cache_prep_kernel, past_len=past_len, tr=tr),
        grid=(n_streams, nr),
        in_specs=[row_spec(D_MODEL), row_spec(D_MODEL),
                  pl.BlockSpec((past_len, LANES), lambda n, r: (n, 0)),
                  _const_spec((LANES, GROUPS))],
        out_specs=[row_spec(GROUPS), row_spec(GROUPS)],
        out_shape=[jax.ShapeDtypeStruct((n_streams * past_len, GROUPS), BF16)] * 2,
        scratch_shapes=[pltpu.VMEM((past_len, LANES), F32)],
        compiler_params=_params(2),
        name="cache_prep",
    )(ck, cv, cl, pmat_k)


def _attend(q, k, v, m, acc, mask=None):
    s = lax.dot_general(q, k, (((1,), (1,)), ((), ())), preferred_element_type=F32)
    if mask is not None:
        s = jnp.where(mask, s, NEG)
    m_new = jnp.maximum(m, jnp.max(s, axis=-1, keepdims=True))
    p = jnp.exp2(s - m_new).astype(BF16)
    acc = jnp.exp2(m - m_new) * acc + _dot(p, v)
    return m_new, acc


def _causal_mask(n):
    return (lax.broadcasted_iota(jnp.int32, (n, n), 1)
            <= lax.broadcasted_iota(jnp.int32, (n, n), 0))


def _finish(accs):
    outs = [a[:, :HEAD_DIM] / a[:, HEAD_DIM:HEAD_DIM + 1] for a in accs]
    return jnp.concatenate(outs, axis=-1)


def _prompt_attn_kernel(trips_ref, qa_ref, ka_ref, va_ref, o_ref, *, tq):
    hp = pl.program_id(0)
    i = pl.program_id(1)
    nq = pl.num_programs(1)
    mask = _causal_mask(tq)
    sls = [slice(e * LANES, (e + 1) * LANES) for e in range(2)]
    qs = [qa_ref[:, sl] for sl in sls]
    n = [trips_ref[(2 * hp + e) * nq + i] for e in range(2)]

    def step(e, j, carry):
        rows = pl.ds(pl.multiple_of(j * tq, tq), tq)
        return _attend(qs[e], ka_ref[rows, sls[e]], va_ref[rows, sls[e]], *carry)

    m0 = jnp.full((tq, 1), NEG, F32)
    acc0 = jnp.zeros((tq, LANES), F32)
    diag = pl.ds(pl.multiple_of(i * tq, tq), tq)
    carry = tuple(_attend(qs[e], ka_ref[diag, sls[e]], va_ref[diag, sls[e]], m0, acc0, mask)
                  for e in range(2))

    def both(jj, carry):
        return tuple(step(e, i - 1 - jj, carry[e]) for e in range(2))

    n_both = jnp.minimum(n[0], n[1])
    carry = lax.fori_loop(0, n_both, both, carry)
    accs = []
    for e in range(2):
        _, acc = lax.fori_loop(n_both, n[e], lambda jj, c, e=e: step(e, i - 1 - jj, c), carry[e])
        accs.append(acc)
    o_ref[...] = _finish(accs)


def _block_trips(stats, *, tiles_per_block):
    st = stats[:, :, :N_HEADS]
    nb = st.shape[0] // tiles_per_block
    per_block = lambda r: st[:, r].reshape(nb, tiles_per_block, N_HEADS)
    qn = per_block(0).max(axis=1)
    kn = per_block(1).max(axis=1)
    cmax = per_block(2).max(axis=1)
    kpre = per_block(3)[:, -1]
    cpre = per_block(4)[:, -1]
    bound = (NORM_SLACK * qn[:, None] * (kpre[None, :] + kn[:, None])
             + cmax[:, None] - cpre[None, :] + BOUND_SLACK)
    behind = jnp.arange(nb)[None, :] < jnp.arange(nb)[:, None]
    needed = (bound > -ZERO_WEIGHT_GAP) & behind[:, :, None]
    return needed.sum(axis=1).astype(jnp.int32).T.reshape(-1)


def _prompt_attn(trips, qa, ka, va, *, tq):
    rows = qa.shape[0]
    pair = 2 * LANES
    return pl.pallas_call(
        functools.partial(_prompt_attn_kernel, tq=tq),
        grid_spec=pltpu.PrefetchScalarGridSpec(
            num_scalar_prefetch=1,
            grid=(N_HEADS // 2, rows // tq),
            in_specs=[pl.BlockSpec((tq, pair), lambda hp, i, t: (i, hp)),
                      pl.BlockSpec((rows, pair), lambda hp, i, t: (0, hp)),
                      pl.BlockSpec((rows, pair), lambda hp, i, t: (0, hp))],
            out_specs=pl.BlockSpec((tq, LANES), lambda hp, i, t: (i, hp))),
        out_shape=jax.ShapeDtypeStruct((rows, D_MODEL), F32),
        compiler_params=_params(2),
        name="prompt_attn",
    )(trips, qa, ka, va)


def _sample_attn_kernel(qa_ref, kn_ref, vn_ref, kc_ref, vc_ref, o_ref):
    t = qa_ref.shape[0]
    mask = _causal_mask(t)
    accs = []
    for e in range(2):
        sl = slice(e * LANES, (e + 1) * LANES)
        q = qa_ref[:, sl]
        m0 = jnp.full((t, 1), NEG, F32)
        acc0 = jnp.zeros((t, LANES), F32)
        m, acc = _attend(q, kn_ref[:, sl], vn_ref[:, sl], m0, acc0, mask)
        _, acc = _attend(q, kc_ref[:, sl], vc_ref[:, sl], m, acc)
        accs.append(acc)
    o_ref[...] = _finish(accs)


def _sample_attn(qa, kn, vn, kc, vc, *, n_streams, seg, past_len):
    pair = 2 * LANES
    new_spec = pl.BlockSpec((seg, pair), lambda n, hp: (n, hp))
    old_spec = pl.BlockSpec((past_len, pair), lambda n, hp: (n, hp))
    return pl.pallas_call(
        _sample_attn_kernel,
        grid=(n_streams, N_HEADS // 2),
        in_specs=[new_spec, new_spec, new_spec, old_spec, old_spec],
        out_specs=pl.BlockSpec((seg, LANES), lambda n, hp: (n, hp)),
        out_shape=jax.ShapeDtypeStruct((n_streams * seg, D_MODEL), F32),
        compiler_params=_params(2),
        name="sample_attn",
    )(qa, kn, vn, kc, vc)


def _out_proj_kernel(x_ref, o_ref, w_ref, y_ref):
    y_ref[...] = x_ref[...] + _dot(o_ref[...].astype(BF16), w_ref[...])


def _out_proj(x, o, w, *, tm):
    rows = x.shape[0]
    spec = pl.BlockSpec((tm, D_MODEL), lambda i: (i, 0))
    return pl.pallas_call(
        _out_proj_kernel,
        grid=(rows // tm,),
        in_specs=[spec, spec, _const_spec((D_MODEL, D_MODEL))],
        out_specs=spec,
        out_shape=jax.ShapeDtypeStruct((rows, D_MODEL), F32),
        compiler_params=_params(1),
        name="out_proj",
    )(x, o, w)


def _trunk(x, conv_past, cache, w, *, n_streams, seg, tm, tq):
    depth = w["mlp_w1"].shape[0]
    conv_states, ks, vs, lfs = [], [], [], []
    carried = n_streams == 1
    for i in range(depth):
        j = i // 2
        g_mix = w["norm_mix"][i:i + 1]
        if i % 2 == 0:
            x, st = _conv_mixer(x, conv_past[j], g_mix, w["conv_w_in"][j], w["conv_w"][j],
                                w["conv_w_out"][j], n_streams=n_streams,
                                seg=tm if carried else seg)
            conv_states.append(st)
        else:
            tp = min(tm, 256)
            k, v, lf, qa, ka, va, stats = _fox_proj(
                x, g_mix, w["fox_wqkv"][j], w["fox_wf"][j], w["fox_bf"][j], w["pmat"], w["gmat"],
                tm=tp, seg=tp if carried else seg, carried=carried)
            if cache is None:
                o = _prompt_attn(_block_trips(stats, tiles_per_block=tq // tp), qa, ka, va, tq=tq)
            else:
                ck, cv, cl = cache
                past_len = ck.shape[2]
                flat = n_streams * past_len
                kc, vc = _cache_prep(ck[j].reshape(flat, D_MODEL), cv[j].reshape(flat, D_MODEL),
                                     cl[j], w["pmat"][:, GROUPS:], n_streams=n_streams,
                                     past_len=past_len, tr=512)
                o = _sample_attn(qa, ka, va, kc, vc, n_streams=n_streams, seg=seg,
                                 past_len=past_len)
            x = _out_proj(x, o, w["fox_w_out"][j], tm=tm)
            ks.append(k)
            vs.append(v)
            lfs.append(lf)
        x = _mlp(x, w["norm_mlp"][i:i + 1], w["mlp_w1"][i], w["mlp_w2"][i], w["norm_final"],
                 tm=tm, final_norm=i == depth - 1)
    return x, jnp.stack(conv_states), jnp.stack(ks), jnp.stack(vs), jnp.stack(lfs)


def kernel(x_prompt, x_sample, state_conv, cache_k, cache_v, cache_logf, norm_mix, norm_mlp,
           norm_final, conv_w_in, conv_w, conv_w_out, fox_w_in, fox_b_f, fox_w_out, mlp_w1, mlp_w2):
    d = D_MODEL
    n_fox = fox_w_in.shape[0]
    batch, seq, _ = x_prompt.shape
    dec_batch, dec_seq, _ = x_sample.shape
    past_len = cache_k.shape[2]
    assert batch == 1

    lane_pad = ((0, 0), (0, 0), (0, LANES - N_HEADS))
    w = dict(
        norm_mix=norm_mix, norm_mlp=norm_mlp, norm_final=norm_final.reshape(1, d),
        conv_w_in=conv_w_in.astype(BF16), conv_w=conv_w, conv_w_out=conv_w_out.astype(BF16),
        fox_wqkv=fox_w_in[:, :, :3 * d].astype(BF16),
        fox_wf=jnp.pad(fox_w_in[:, :, 3 * d:], lane_pad).astype(BF16),
        fox_bf=jnp.pad(fox_b_f.reshape(n_fox, 1, N_HEADS), lane_pad),
        fox_w_out=fox_w_out.astype(BF16),
        mlp_w1=mlp_w1.astype(BF16), mlp_w2=mlp_w2.astype(BF16),
        pmat=jnp.asarray(_placement_matrix(), BF16),
        gmat=jnp.asarray(np.repeat(np.eye(N_HEADS, LANES, dtype=np.float32), HEAD_DIM, axis=0), BF16),
    )

    zero_conv = jnp.zeros((conv_w.shape[0], batch, CONV_WIDTH - 1, d), F32)
    y_p, p_conv, p_k, p_v, p_lf = _trunk(
        x_prompt.reshape(seq, d), zero_conv, None, w, n_streams=1, seg=seq, tm=512, tq=512)

    cl = jnp.pad(cache_logf.reshape(n_fox, dec_batch * past_len, N_HEADS), lane_pad)
    y_s, s_conv, s_k, s_v, s_lf = _trunk(
        x_sample.reshape(dec_batch * dec_seq, d), state_conv, (cache_k, cache_v, cl), w,
        n_streams=dec_batch, seg=dec_seq, tm=dec_batch * dec_seq, tq=None)

    heads = (N_HEADS, HEAD_DIM)
    return (y_p.reshape(batch, seq, d), y_s.reshape(dec_batch, dec_seq, d),
            p_conv,
            p_k.reshape(n_fox, batch, seq, *heads), p_v.reshape(n_fox, batch, seq, *heads),
            p_lf.reshape(n_fox, batch, seq, N_HEADS),
            s_conv,
            s_k.reshape(n_fox, dec_batch, dec_seq, *heads),
            s_v.reshape(n_fox, dec_batch, dec_seq, *heads),
            s_lf.reshape(n_fox, dec_batch, dec_seq, N_HEADS))
```

```python
import functools

import numpy as np
import jax
import jax.numpy as jnp
from jax import lax
from jax.experimental import pallas as pl
from jax.experimental.pallas import tpu as pltpu

D_MODEL = 1024
N_HEADS = 16
HEAD_DIM = 64
D_FF = 4 * D_MODEL
CONV_WIDTH = 3
EPS = 1e-5
NEG = -1e30

LANES = 128
GROUPS = N_HEADS * LANES
CARRY_ROWS = 8
VMEM_LIMIT = 56 * 1024 * 1024

F32 = jnp.float32
BF16 = jnp.bfloat16

LOG2E = 1.4426950408889634
Q_SCALE = HEAD_DIM ** -0.5 * LOG2E
BIG = 3.0e38
ZERO_WEIGHT_GAP = 150.0
NORM_SLACK = 1.02
BOUND_SLACK = 1.0


def _rms(x, g):
    r = lax.rsqrt(jnp.mean(x * x, axis=-1, keepdims=True) + EPS)
    return x * r * g


def _log_sigmoid(x):
    return jnp.minimum(x, 0.0) - jnp.log1p(jnp.exp(-jnp.abs(x)))


def _dot(a, b):
    return jnp.dot(a, b, preferred_element_type=F32)


def _const_spec(shape):
    return pl.BlockSpec(shape, lambda *_: (0,) * len(shape), pipeline_mode=pl.Buffered(1))


def _params(n_axes):
    return pltpu.CompilerParams(dimension_semantics=("arbitrary",) * n_axes,
                                vmem_limit_bytes=VMEM_LIMIT)


MLP_CHUNKS = 4


def _mlp_block(x, g_ref, w1_ref, w2_ref, gf_ref, final_norm):
    h = _rms(x, g_ref[...]).astype(BF16)
    ck = D_FF // MLP_CHUNKS
    acc = x
    for c in range(MLP_CHUNKS):
        a = _dot(h, w1_ref[:, c * ck:(c + 1) * ck])
        a = jnp.square(jnp.maximum(a, 0.0)).astype(BF16)
        acc = acc + _dot(a, w2_ref[c * ck:(c + 1) * ck, :])
    if final_norm:
        acc = _rms(acc, gf_ref[...])
    return acc


def _mlp_specs():
    return [_const_spec((1, D_MODEL)), _const_spec((D_MODEL, D_FF)),
            _const_spec((D_FF, D_MODEL)), _const_spec((1, D_MODEL))]


def _conv_kernel(x_ref, past_ref, g_ref, win_ref, cw_ref, wout_ref,
                 gm_ref, w1_ref, w2_ref, gf_ref, y_ref, st_ref, up_ref,
                 *, n_streams, seg, final_norm):
    d = D_MODEL
    lo = CARRY_ROWS - (CONV_WIDTH - 1)

    @pl.when(pl.program_id(0) == 0)
    def _():
        up_ref[:, lo:CARRY_ROWS, :] = past_ref[...]

    x = x_ref[...]
    h = _rms(x, g_ref[...]).astype(BF16)
    b = _dot(h, win_ref[:, :d])
    u = _dot(h, win_ref[:, d:2 * d]) * _dot(h, win_ref[:, 2 * d:])
    u3 = u.reshape(n_streams, seg, d)
    up_ref[:, CARRY_ROWS:CARRY_ROWS + seg, :] = u3
    conv = cw_ref[2:3, :].reshape(1, 1, d) * u3
    for j in range(CONV_WIDTH - 1):
        conv = conv + cw_ref[j:j + 1, :].reshape(1, 1, d) * up_ref[:, lo + j:lo + j + seg, :]
    gated = (b * conv.reshape(n_streams * seg, d)).astype(BF16)
    y_ref[...] = _mlp_block(x + _dot(gated, wout_ref[...]), gm_ref, w1_ref, w2_ref, gf_ref,
                            final_norm)
    state = up_ref[:, CARRY_ROWS + seg - (CONV_WIDTH - 1):CARRY_ROWS + seg, :]
    st_ref[...] = state
    up_ref[:, lo:CARRY_ROWS, :] = state


def _conv_mixer(x, past, g, w_in, cw, w_out, mlp, *, n_streams, seg, final_norm):
    rows = x.shape[0]
    tm = n_streams * seg
    assert rows % tm == 0 and (rows == tm or n_streams == 1)
    state_shape = (n_streams, CONV_WIDTH - 1, D_MODEL)
    return pl.pallas_call(
        functools.partial(_conv_kernel, n_streams=n_streams, seg=seg, final_norm=final_norm),
        grid=(rows // tm,),
        in_specs=[pl.BlockSpec((tm, D_MODEL), lambda i: (i, 0)),
                  _const_spec(state_shape),
                  _const_spec((1, D_MODEL)),
                  _const_spec((D_MODEL, 3 * D_MODEL)),
                  _const_spec((CONV_WIDTH, D_MODEL)),
                  _const_spec((D_MODEL, D_MODEL))] + _mlp_specs(),
        out_specs=[pl.BlockSpec((tm, D_MODEL), lambda i: (i, 0)),
                   pl.BlockSpec(state_shape, lambda i: (0, 0, 0))],
        out_shape=[jax.ShapeDtypeStruct((rows, D_MODEL), F32),
                   jax.ShapeDtypeStruct(state_shape, F32)],
        scratch_shapes=[pltpu.VMEM((n_streams, CARRY_ROWS + seg, D_MODEL), F32)],
        compiler_params=_params(1),
        name="conv_mixer",
    )(x, past, g, w_in, cw, w_out, *mlp)


def _segment_cumsum(x, seg):
    t = lax.broadcasted_iota(jnp.int32, x.shape, 0) & (seg - 1)
    step = 1
    while step < seg:
        x = x + jnp.where(t >= step, pltpu.roll(x, step, axis=0), 0.0)
        step *= 2
    return x


def _split3(c):
    c1 = c.astype(BF16).astype(F32)
    r1 = c - c1
    c2 = r1.astype(BF16).astype(F32)
    c3 = (r1 - c2).astype(BF16).astype(F32)
    lane = lax.broadcasted_iota(jnp.int32, c.shape, 1)
    packed = (c1 + pltpu.roll(c2, N_HEADS, axis=1) + pltpu.roll(c3, 2 * N_HEADS, axis=1)
              + jnp.where(lane == 3 * N_HEADS, 1.0, 0.0))
    return packed.astype(BF16)


def _placement_matrix():
    p = np.zeros((LANES, 2 * GROUPS), np.float32)
    one = 3 * N_HEADS
    for h in range(N_HEADS):
        qb = h * LANES + HEAD_DIM
        kb = GROUPS + h * LANES + HEAD_DIM
        for s in range(3):
            p[s * N_HEADS + h, qb + s] = 1.0
            p[one, qb + 3 + s] = 1.0
            p[one, kb + s] = 1.0
            p[s * N_HEADS + h, kb + 3 + s] = -1.0
    return p


def _store_groups(z, extra, ref):
    rows = z.shape[0]
    lane = lax.broadcasted_iota(jnp.int32, (rows, LANES), 1)
    low = lane < HEAD_DIM
    ones_col = jnp.where(lane == HEAD_DIM, 1.0, 0.0)
    for p in range(N_HEADS // 2):
        zc = z[:, p * LANES:(p + 1) * LANES]
        halves = (zc, pltpu.roll(zc, HEAD_DIM, axis=1))
        for e in range(2):
            h = 2 * p + e
            fill = ones_col if extra is None else extra[:, h * LANES:(h + 1) * LANES]
            ref[:, h * LANES:(h + 1) * LANES] = jnp.where(low, halves[e], fill).astype(BF16)


def _head_norms(z, gmat):
    zb = z.astype(BF16).astype(F32)
    return jnp.sqrt(_dot((zb * zb).astype(BF16), gmat))


def _fox_proj_kernel(x_ref, g_ref, wqkv_ref, wf_ref, bf_ref, pmat_ref, gmat_ref,
                     k_ref, v_ref, lf_ref, qa_ref, ka_ref, va_ref, st_ref, carry_ref,
                     *, seg, carried):
    d = D_MODEL
    rows = x_ref.shape[0]

    @pl.when(pl.program_id(0) == 0)
    def _():
        carry_ref[...] = jnp.zeros_like(carry_ref)
        carry_ref[2:3, :] = jnp.full((1, LANES), BIG, F32)

    h = _rms(x_ref[...], g_ref[...]).astype(BF16)
    q = _dot(h, wqkv_ref[:, :d]) * Q_SCALE
    k = _dot(h, wqkv_ref[:, d:2 * d])
    v = _dot(h, wqkv_ref[:, 2 * d:])
    zf = _dot(h, wf_ref[...]) + bf_ref[...]
    lane = lax.broadcasted_iota(jnp.int32, (rows, LANES), 1)
    lf = jnp.where(lane < N_HEADS, _log_sigmoid(zf), 0.0)
    k_ref[...] = k
    v_ref[...] = v
    lf_ref[...] = lf[:, :N_HEADS]

    c = _segment_cumsum(lf, seg)
    if carried:
        c = c + carry_ref[0:1, :]
        carry_ref[0:1, :] = c[rows - 1:rows, :]
    c = c * LOG2E
    aug = _dot(_split3(c), pmat_ref[...])
    _store_groups(q, aug[:, :GROUPS], qa_ref)
    _store_groups(k, aug[:, GROUPS:], ka_ref)
    _store_groups(v, None, va_ref)

    k_max = jnp.max(_head_norms(k, gmat_ref[...]), axis=0, keepdims=True)
    k_pre = jnp.maximum(carry_ref[1:2, :], k_max)
    c_pre = jnp.minimum(carry_ref[2:3, :], jnp.min(c, axis=0, keepdims=True))
    carry_ref[1:2, :] = k_pre
    carry_ref[2:3, :] = c_pre
    st_ref[0, 0:1, :] = jnp.max(_head_norms(q, gmat_ref[...]), axis=0, keepdims=True)
    st_ref[0, 1:2, :] = k_max
    st_ref[0, 2:3, :] = jnp.max(c, axis=0, keepdims=True)
    st_ref[0, 3:4, :] = k_pre
    st_ref[0, 4:5, :] = c_pre
    st_ref[0, 5:8, :] = jnp.zeros((3, LANES), F32)


def _fox_proj(x, g, wqkv, wf, bf, pmat, gmat, *, tm, seg, carried):
    rows = x.shape[0]
    assert rows % tm == 0 and tm % seg == 0 and (not carried or seg == tm)
    row_spec = lambda w: pl.BlockSpec((tm, w), lambda i: (i, 0))
    return pl.pallas_call(
        functools.partial(_fox_proj_kernel, seg=seg, carried=carried),
        grid=(rows // tm,),
        in_specs=[row_spec(D_MODEL),
                  _const_spec((1, D_MODEL)),
                  _const_spec((D_MODEL, 3 * D_MODEL)),
                  _const_spec((D_MODEL, LANES)),
                  _const_spec((1, LANES)),
                  _const_spec((LANES, 2 * GROUPS)),
                  _const_spec((D_MODEL, LANES))],
        out_specs=[row_spec(D_MODEL), row_spec(D_MODEL), row_spec(N_HEADS),
                   row_spec(GROUPS), row_spec(GROUPS), row_spec(GROUPS),
                   pl.BlockSpec((1, CARRY_ROWS, LANES), lambda i: (i, 0, 0))],
        out_shape=[jax.ShapeDtypeStruct((rows, D_MODEL), F32),
                   jax.ShapeDtypeStruct((rows, D_MODEL), F32),
                   jax.ShapeDtypeStruct((rows, N_HEADS), F32),
                   jax.ShapeDtypeStruct((rows, GROUPS), BF16),
                   jax.ShapeDtypeStruct((rows, GROUPS), BF16),
                   jax.ShapeDtypeStruct((rows, GROUPS), BF16),
                   jax.ShapeDtypeStruct((rows // tm, CARRY_ROWS, LANES), F32)],
        scratch_shapes=[pltpu.VMEM((CARRY_ROWS, LANES), F32)],
        compiler_params=_params(1),
        name="fox_proj",
    )(x, g, wqkv, wf, bf, pmat, gmat)


def _cache_prep_kernel(ck_ref, cv_ref, cl_ref, pmat_ref, ka_ref, va_ref, c_ref, *, past_len, tr):
    r = pl.program_id(1)

    @pl.when(r == 0)
    def _():
        c = _segment_cumsum(cl_ref[...], past_len)
        c_ref[...] = (c - c[past_len - 1:past_len, :]) * LOG2E

    c = c_ref[pl.ds(pl.multiple_of(r * tr, tr), tr), :]
    aug = _dot(_split3(c), pmat_ref[...])
    _store_groups(ck_ref[...], aug, ka_ref)
    _store_groups(cv_ref[...], None, va_ref)


def _cache_prep(ck, cv, cl, pmat_k, *, n_streams, past_len, tr):
    nr = past_len // tr
    row_spec = lambda w: pl.BlockSpec((tr, w), lambda n, r: (n * nr + r, 0))
    return pl.pallas_call(
        functools.partial(_cache_prep_kernel, past_len=past_len, tr=tr),
        grid=(n_streams, nr),
        in_specs=[row_spec(D_MODEL), row_spec(D_MODEL),
                  pl.BlockSpec((past_len, LANES), lambda n, r: (n, 0)),
                  _const_spec((LANES, GROUPS))],
        out_specs=[row_spec(GROUPS), row_spec(GROUPS)],
        out_shape=[jax.ShapeDtypeStruct((n_streams * past_len, GROUPS), BF16)] * 2,
        scratch_shapes=[pltpu.VMEM((past_len, LANES), F32)],
        compiler_params=_params(2),
        name="cache_prep",
    )(ck, cv, cl, pmat_k)


def _attend(q, k, v, m, acc, mask=None):
    s = lax.dot_general(q, k, (((1,), (1,)), ((), ())), preferred_element_type=F32)
    if mask is not None:
        s = jnp.where(mask, s, NEG)
    m_new = jnp.maximum(m, jnp.max(s, axis=-1, keepdims=True))
    p = jnp.exp2(s - m_new).astype(BF16)
    acc = jnp.exp2(m - m_new) * acc + _dot(p, v)
    return m_new, acc


def _causal_mask(n):
    return (lax.broadcasted_iota(jnp.int32, (n, n), 1)
            <= lax.broadcasted_iota(jnp.int32, (n, n), 0))


def _finish(accs):
    outs = [a[:, :HEAD_DIM] / a[:, HEAD_DIM:HEAD_DIM + 1] for a in accs]
    return jnp.concatenate(outs, axis=-1)


def _prompt_attn_kernel(trips_ref, qa_ref, ka_ref, va_ref, o_ref, *, tq):
    hp = pl.program_id(0)
    i = pl.program_id(1)
    nq = pl.num_programs(1)
    mask = _causal_mask(tq)
    sls = [slice(e * LANES, (e + 1) * LANES) for e in range(2)]
    qs = [qa_ref[:, sl] for sl in sls]
    n = [trips_ref[(2 * hp + e) * nq + i] for e in range(2)]

    def step(e, j, carry):
        rows = pl.ds(pl.multiple_of(j * tq, tq), tq)
        return _attend(qs[e], ka_ref[rows, sls[e]], va_ref[rows, sls[e]], *carry)

    m0 = jnp.full((tq, 1), NEG, F32)
    acc0 = jnp.zeros((tq, LANES), F32)
    diag = pl.ds(pl.multiple_of(i * tq, tq), tq)
    carry = tuple(_attend(qs[e], ka_ref[diag, sls[e]], va_ref[diag, sls[e]], m0, acc0, mask)
                  for e in range(2))

    def both(jj, carry):
        return tuple(step(e, i - 1 - jj, carry[e]) for e in range(2))

    n_both = jnp.minimum(n[0], n[1])
    carry = lax.fori_loop(0, n_both, both, carry)
    accs = []
    for e in range(2):
        _, acc = lax.fori_loop(n_both, n[e], lambda jj, c, e=e: step(e, i - 1 - jj, c), carry[e])
        accs.append(acc)
    o_ref[...] = _finish(accs)


def _block_trips(stats, *, tiles_per_block):
    st = stats[:, :, :N_HEADS]
    nb = st.shape[0] // tiles_per_block
    per_block = lambda r: st[:, r].reshape(nb, tiles_per_block, N_HEADS)
    qn = per_block(0).max(axis=1)
    kn = per_block(1).max(axis=1)
    cmax = per_block(2).max(axis=1)
    kpre = per_block(3)[:, -1]
    cpre = per_block(4)[:, -1]
    bound = (NORM_SLACK * qn[:, None] * (kpre[None, :] + kn[:, None])
             + cmax[:, None] - cpre[None, :] + BOUND_SLACK)
    behind = jnp.arange(nb)[None, :] < jnp.arange(nb)[:, None]
    needed = (bound > -ZERO_WEIGHT_GAP) & behind[:, :, None]
    return needed.sum(axis=1).astype(jnp.int32).T.reshape(-1)


def _prompt_attn(trips, qa, ka, va, *, tq):
    rows = qa.shape[0]
    pair = 2 * LANES
    return pl.pallas_call(
        functools.partial(_prompt_attn_kernel, tq=tq),
        grid_spec=pltpu.PrefetchScalarGridSpec(
            num_scalar_prefetch=1,
            grid=(N_HEADS // 2, rows // tq),
            in_specs=[pl.BlockSpec((tq, pair), lambda hp, i, t: (i, hp)),
                      pl.BlockSpec((rows, pair), lambda hp, i, t: (0, hp)),
                      pl.BlockSpec((rows, pair), lambda hp, i, t: (0, hp))],
            out_specs=pl.BlockSpec((tq, LANES), lambda hp, i, t: (i, hp))),
        out_shape=jax.ShapeDtypeStruct((rows, D_MODEL), F32),
        compiler_params=_params(2),
        name="prompt_attn",
    )(trips, qa, ka, va)


def _sample_attn_kernel(qa_ref, kn_ref, vn_ref, kc_ref, vc_ref, o_ref):
    t = qa_ref.shape[0]
    mask = _causal_mask(t)
    accs = []
    for e in range(2):
        sl = slice(e * LANES, (e + 1) * LANES)
        q = qa_ref[:, sl]
        m0 = jnp.full((t, 1), NEG, F32)
        acc0 = jnp.zeros((t, LANES), F32)
        m, acc = _attend(q, kn_ref[:, sl], vn_ref[:, sl], m0, acc0, mask)
        _, acc = _attend(q, kc_ref[:, sl], vc_ref[:, sl], m, acc)
        accs.append(acc)
    o_ref[...] = _finish(accs)


def _sample_attn(qa, kn, vn, kc, vc, *, n_streams, seg, past_len):
    pair = 2 * LANES
    new_spec = pl.BlockSpec((seg, pair), lambda n, hp: (n, hp))
    old_spec = pl.BlockSpec((past_len, pair), lambda n, hp: (n, hp))
    return pl.pallas_call(
        _sample_attn_kernel,
        grid=(n_streams, N_HEADS // 2),
        in_specs=[new_spec, new_spec, new_spec, old_spec, old_spec],
        out_specs=pl.BlockSpec((seg, LANES), lambda n, hp: (n, hp)),
        out_shape=jax.ShapeDtypeStruct((n_streams * seg, D_MODEL), F32),
        compiler_params=_params(2),
        name="sample_attn",
    )(qa, kn, vn, kc, vc)


def _out_proj_kernel(x_ref, o_ref, w_ref, gm_ref, w1_ref, w2_ref, gf_ref, y_ref, *, final_norm):
    y = x_ref[...] + _dot(o_ref[...].astype(BF16), w_ref[...])
    y_ref[...] = _mlp_block(y, gm_ref, w1_ref, w2_ref, gf_ref, final_norm)


def _out_proj(x, o, w, mlp, *, tm, final_norm):
    rows = x.shape[0]
    spec = pl.BlockSpec((tm, D_MODEL), lambda i: (i, 0))
    return pl.pallas_call(
        functools.partial(_out_proj_kernel, final_norm=final_norm),
        grid=(rows // tm,),
        in_specs=[spec, spec, _const_spec((D_MODEL, D_MODEL))] + _mlp_specs(),
        out_specs=spec,
        out_shape=jax.ShapeDtypeStruct((rows, D_MODEL), F32),
        compiler_params=_params(1),
        name="out_proj",
    )(x, o, w, *mlp)


def _trunk(x, conv_past, cache, w, *, n_streams, seg, tm, tq):
    depth = w["mlp_w1"].shape[0]
    conv_states, ks, vs, lfs = [], [], [], []
    carried = n_streams == 1
    for i in range(depth):
        j = i // 2
        g_mix = w["norm_mix"][i:i + 1]
        mlp = (w["norm_mlp"][i:i + 1], w["mlp_w1"][i], w["mlp_w2"][i], w["norm_final"])
        last = i == depth - 1
        if i % 2 == 0:
            x, st = _conv_mixer(x, conv_past[j], g_mix, w["conv_w_in"][j], w["conv_w"][j],
                                w["conv_w_out"][j], mlp, n_streams=n_streams,
                                seg=tm if carried else seg, final_norm=last)
            conv_states.append(st)
        else:
            tp = min(tm, 256)
            k, v, lf, qa, ka, va, stats = _fox_proj(
                x, g_mix, w["fox_wqkv"][j], w["fox_wf"][j], w["fox_bf"][j], w["pmat"], w["gmat"],
                tm=tp, seg=tp if carried else seg, carried=carried)
            if cache is None:
                o = _prompt_attn(_block_trips(stats, tiles_per_block=tq // tp), qa, ka, va, tq=tq)
            else:
                ck, cv, cl = cache
                past_len = ck.shape[2]
                flat = n_streams * past_len
                kc, vc = _cache_prep(ck[j].reshape(flat, D_MODEL), cv[j].reshape(flat, D_MODEL),
                                     cl[j], w["pmat"][:, GROUPS:], n_streams=n_streams,
                                     past_len=past_len, tr=512)
                o = _sample_attn(qa, ka, va, kc, vc, n_streams=n_streams, seg=seg,
                                 past_len=past_len)
            x = _out_proj(x, o, w["fox_w_out"][j], mlp, tm=tm, final_norm=last)
            ks.append(k)
            vs.append(v)
            lfs.append(lf)
    return x, jnp.stack(conv_states), jnp.stack(ks), jnp.stack(vs), jnp.stack(lfs)


def kernel(x_prompt, x_sample, state_conv, cache_k, cache_v, cache_logf, norm_mix, norm_mlp,
           norm_final, conv_w_in, conv_w, conv_w_out, fox_w_in, fox_b_f, fox_w_out, mlp_w1, mlp_w2):
    d = D_MODEL
    n_fox = fox_w_in.shape[0]
    batch, seq, _ = x_prompt.shape
    dec_batch, dec_seq, _ = x_sample.shape
    past_len = cache_k.shape[2]
    assert batch == 1

    lane_pad = ((0, 0), (0, 0), (0, LANES - N_HEADS))
    w = dict(
        norm_mix=norm_mix, norm_mlp=norm_mlp, norm_final=norm_final.reshape(1, d),
        conv_w_in=conv_w_in.astype(BF16), conv_w=conv_w, conv_w_out=conv_w_out.astype(BF16),
        fox_wqkv=fox_w_in[:, :, :3 * d].astype(BF16),
        fox_wf=jnp.pad(fox_w_in[:, :, 3 * d:], lane_pad).astype(BF16),
        fox_bf=jnp.pad(fox_b_f.reshape(n_fox, 1, N_HEADS), lane_pad),
        fox_w_out=fox_w_out.astype(BF16),
        mlp_w1=mlp_w1.astype(BF16), mlp_w2=mlp_w2.astype(BF16),
        pmat=jnp.asarray(_placement_matrix(), BF16),
        gmat=jnp.asarray(np.repeat(np.eye(N_HEADS, LANES, dtype=np.float32), HEAD_DIM, axis=0), BF16),
    )

    zero_conv = jnp.zeros((conv_w.shape[0], batch, CONV_WIDTH - 1, d), F32)
    y_p, p_conv, p_k, p_v, p_lf = _trunk(
        x_prompt.reshape(seq, d), zero_conv, None, w, n_streams=1, seg=seq, tm=512, tq=512)

    cl = jnp.pad(cache_logf.reshape(n_fox, dec_batch * past_len, N_HEADS), lane_pad)
    y_s, s_conv, s_k, s_v, s_lf = _trunk(
        x_sample.reshape(dec_batch * dec_seq, d), state_conv, (cache_k, cache_v, cl), w,
        n_streams=dec_batch, seg=dec_seq, tm=dec_batch * dec_seq, tq=None)

    heads = (N_HEADS, HEAD_DIM)
    return (y_p.reshape(batch, seq, d), y_s.reshape(dec_batch, dec_seq, d),
            p_conv,
            p_k.reshape(n_fox, batch, seq, *heads), p_v.reshape(n_fox, batch, seq, *heads),
            p_lf.reshape(n_fox, batch, seq, N_HEADS),
            s_conv,
            s_k.reshape(n_fox, dec_batch, dec_seq, *heads),
            s_v.reshape(n_fox, dec_batch, dec_seq, *heads),
            s_lf.reshape(n_fox, dec_batch, dec_seq, N_HEADS))
```

```python
import functools

import numpy as np
import jax
import jax.numpy as jnp
from jax import lax
from jax.experimental import pallas as pl
from jax.experimental.pallas import tpu as pltpu

D_MODEL = 1024
N_HEADS = 16
HEAD_DIM = 64
D_FF = 4 * D_MODEL
CONV_WIDTH = 3
EPS = 1e-5
NEG = -1e30

LANES = 128
GROUPS = N_HEADS * LANES
CARRY_ROWS = 8
VMEM_LIMIT = 56 * 1024 * 1024

F32 = jnp.float32
BF16 = jnp.bfloat16

LOG2E = 1.4426950408889634
Q_SCALE = HEAD_DIM ** -0.5 * LOG2E
BIG = 3.0e38
ZERO_WEIGHT_GAP = 150.0
NORM_SLACK = 1.02
BOUND_SLACK = 1.0


def _rms(x, g):
    r = lax.rsqrt(jnp.mean(x * x, axis=-1, keepdims=True) + EPS)
    return x * r * g


def _log_sigmoid(x):
    return jnp.minimum(x, 0.0) - jnp.log1p(jnp.exp(-jnp.abs(x)))


def _dot(a, b):
    return jnp.dot(a, b, preferred_element_type=F32)


def _const_spec(shape):
    return pl.BlockSpec(shape, lambda *_: (0,) * len(shape), pipeline_mode=pl.Buffered(1))


def _params(n_axes):
    return pltpu.CompilerParams(dimension_semantics=("arbitrary",) * n_axes,
                                vmem_limit_bytes=VMEM_LIMIT)


MLP_CHUNKS = 4


def _mlp_block(x, g_ref, w1_ref, w2_ref, gf_ref, final_norm):
    h = _rms(x, g_ref[...]).astype(BF16)
    ck = D_FF // MLP_CHUNKS
    acc = x
    for c in range(MLP_CHUNKS):
        a = _dot(h, w1_ref[:, c * ck:(c + 1) * ck])
        a = jnp.square(jnp.maximum(a, 0.0)).astype(BF16)
        acc = acc + _dot(a, w2_ref[c * ck:(c + 1) * ck, :])
    if final_norm:
        acc = _rms(acc, gf_ref[...])
    return acc


def _mlp_specs():
    return [_const_spec((1, D_MODEL)), _const_spec((D_MODEL, D_FF)),
            _const_spec((D_FF, D_MODEL)), _const_spec((1, D_MODEL))]


def _conv_kernel(x_ref, past_ref, g_ref, win_ref, cw_ref, wout_ref,
                 gm_ref, w1_ref, w2_ref, gf_ref, y_ref, st_ref, up_ref,
                 *, n_streams, seg, final_norm):
    d = D_MODEL
    lo = CARRY_ROWS - (CONV_WIDTH - 1)

    @pl.when(pl.program_id(0) == 0)
    def _():
        up_ref[:, lo:CARRY_ROWS, :] = past_ref[...]

    x = x_ref[...]
    h = _rms(x, g_ref[...]).astype(BF16)
    b = _dot(h, win_ref[:, :d])
    u = _dot(h, win_ref[:, d:2 * d]) * _dot(h, win_ref[:, 2 * d:])
    u3 = u.reshape(n_streams, seg, d)
    up_ref[:, CARRY_ROWS:CARRY_ROWS + seg, :] = u3
    conv = cw_ref[2:3, :].reshape(1, 1, d) * u3
    for j in range(CONV_WIDTH - 1):
        conv = conv + cw_ref[j:j + 1, :].reshape(1, 1, d) * up_ref[:, lo + j:lo + j + seg, :]
    gated = (b * conv.reshape(n_streams * seg, d)).astype(BF16)
    y_ref[...] = _mlp_block(x + _dot(gated, wout_ref[...]), gm_ref, w1_ref, w2_ref, gf_ref,
                            final_norm)
    state = up_ref[:, CARRY_ROWS + seg - (CONV_WIDTH - 1):CARRY_ROWS + seg, :]
    st_ref[...] = state
    up_ref[:, lo:CARRY_ROWS, :] = state


def _conv_mixer(x, past, g, w_in, cw, w_out, mlp, *, n_streams, seg, final_norm):
    rows = x.shape[0]
    tm = n_streams * seg
    assert rows % tm == 0 and (rows == tm or n_streams == 1)
    state_shape = (n_streams, CONV_WIDTH - 1, D_MODEL)
    return pl.pallas_call(
        functools.partial(_conv_kernel, n_streams=n_streams, seg=seg, final_norm=final_norm),
        grid=(rows // tm,),
        in_specs=[pl.BlockSpec((tm, D_MODEL), lambda i: (i, 0)),
                  _const_spec(state_shape),
                  _const_spec((1, D_MODEL)),
                  _const_spec((D_MODEL, 3 * D_MODEL)),
                  _const_spec((CONV_WIDTH, D_MODEL)),
                  _const_spec((D_MODEL, D_MODEL))] + _mlp_specs(),
        out_specs=[pl.BlockSpec((tm, D_MODEL), lambda i: (i, 0)),
                   pl.BlockSpec(state_shape, lambda i: (0, 0, 0))],
        out_shape=[jax.ShapeDtypeStruct((rows, D_MODEL), F32),
                   jax.ShapeDtypeStruct(state_shape, F32)],
        scratch_shapes=[pltpu.VMEM((n_streams, CARRY_ROWS + seg, D_MODEL), F32)],
        compiler_params=_params(1),
        name="conv_mixer",
    )(x, past, g, w_in, cw, w_out, *mlp)


def _segment_cumsum(x, seg):
    t = lax.broadcasted_iota(jnp.int32, x.shape, 0) & (seg - 1)
    step = 1
    while step < seg:
        x = x + jnp.where(t >= step, pltpu.roll(x, step, axis=0), 0.0)
        step *= 2
    return x


def _split3(c):
    c1 = c.astype(BF16).astype(F32)
    r1 = c - c1
    c2 = r1.astype(BF16).astype(F32)
    c3 = (r1 - c2).astype(BF16).astype(F32)
    lane = lax.broadcasted_iota(jnp.int32, c.shape, 1)
    packed = (c1 + pltpu.roll(c2, N_HEADS, axis=1) + pltpu.roll(c3, 2 * N_HEADS, axis=1)
              + jnp.where(lane == 3 * N_HEADS, 1.0, 0.0))
    return packed.astype(BF16)


def _placement_matrix():
    p = np.zeros((LANES, 2 * GROUPS), np.float32)
    one = 3 * N_HEADS
    for h in range(N_HEADS):
        qb = h * LANES + HEAD_DIM
        kb = GROUPS + h * LANES + HEAD_DIM
        for s in range(3):
            p[s * N_HEADS + h, qb + s] = 1.0
            p[one, qb + 3 + s] = 1.0
            p[one, kb + s] = 1.0
            p[s * N_HEADS + h, kb + 3 + s] = -1.0
    return p


def _store_groups(z, extra, ref):
    rows = z.shape[0]
    lane = lax.broadcasted_iota(jnp.int32, (rows, LANES), 1)
    low = lane < HEAD_DIM
    ones_col = jnp.where(lane == HEAD_DIM, 1.0, 0.0)
    for p in range(N_HEADS // 2):
        zc = z[:, p * LANES:(p + 1) * LANES]
        halves = (zc, pltpu.roll(zc, HEAD_DIM, axis=1))
        for e in range(2):
            h = 2 * p + e
            fill = ones_col if extra is None else extra[:, h * LANES:(h + 1) * LANES]
            ref[:, h * LANES:(h + 1) * LANES] = jnp.where(low, halves[e], fill).astype(BF16)


def _head_norms(z, gmat):
    zb = z.astype(BF16).astype(F32)
    return jnp.sqrt(_dot((zb * zb).astype(BF16), gmat))


def _fox_proj_kernel(x_ref, g_ref, wqkv_ref, wf_ref, bf_ref, pmat_ref, gmat_ref,
                     k_ref, v_ref, lf_ref, qa_ref, ka_ref, va_ref, st_ref, carry_ref,
                     *, seg, carried):
    d = D_MODEL
    rows = x_ref.shape[0]

    @pl.when(pl.program_id(0) == 0)
    def _():
        carry_ref[...] = jnp.zeros_like(carry_ref)
        carry_ref[2:3, :] = jnp.full((1, LANES), BIG, F32)

    h = _rms(x_ref[...], g_ref[...]).astype(BF16)
    q = _dot(h, wqkv_ref[:, :d]) * Q_SCALE
    k = _dot(h, wqkv_ref[:, d:2 * d])
    v = _dot(h, wqkv_ref[:, 2 * d:])
    zf = _dot(h, wf_ref[...]) + bf_ref[...]
    lane = lax.broadcasted_iota(jnp.int32, (rows, LANES), 1)
    lf = jnp.where(lane < N_HEADS, _log_sigmoid(zf), 0.0)
    k_ref[...] = k
    v_ref[...] = v
    lf_ref[...] = lf[:, :N_HEADS]

    c = _segment_cumsum(lf, seg)
    if carried:
        c = c + carry_ref[0:1, :]
        carry_ref[0:1, :] = c[rows - 1:rows, :]
    c = c * LOG2E
    aug = _dot(_split3(c), pmat_ref[...])
    _store_groups(q, aug[:, :GROUPS], qa_ref)
    _store_groups(k, aug[:, GROUPS:], ka_ref)
    _store_groups(v, None, va_ref)

    k_max = jnp.max(_head_norms(k, gmat_ref[...]), axis=0, keepdims=True)
    k_pre = jnp.maximum(carry_ref[1:2, :], k_max)
    c_pre = jnp.minimum(carry_ref[2:3, :], jnp.min(c, axis=0, keepdims=True))
    carry_ref[1:2, :] = k_pre
    carry_ref[2:3, :] = c_pre
    st_ref[0, 0:1, :] = jnp.max(_head_norms(q, gmat_ref[...]), axis=0, keepdims=True)
    st_ref[0, 1:2, :] = k_max
    st_ref[0, 2:3, :] = jnp.max(c, axis=0, keepdims=True)
    st_ref[0, 3:4, :] = k_pre
    st_ref[0, 4:5, :] = c_pre
    st_ref[0, 5:8, :] = jnp.zeros((3, LANES), F32)


def _fox_proj(x, g, wqkv, wf, bf, pmat, gmat, *, tm, seg, carried):
    rows = x.shape[0]
    assert rows % tm == 0 and tm % seg == 0 and (not carried or seg == tm)
    row_spec = lambda w: pl.BlockSpec((tm, w), lambda i: (i, 0))
    return pl.pallas_call(
        functools.partial(_fox_proj_kernel, seg=seg, carried=carried),
        grid=(rows // tm,),
        in_specs=[row_spec(D_MODEL),
                  _const_spec((1, D_MODEL)),
                  _const_spec((D_MODEL, 3 * D_MODEL)),
                  _const_spec((D_MODEL, LANES)),
                  _const_spec((1, LANES)),
                  _const_spec((LANES, 2 * GROUPS)),
                  _const_spec((D_MODEL, LANES))],
        out_specs=[row_spec(D_MODEL), row_spec(D_MODEL), row_spec(N_HEADS),
                   row_spec(GROUPS), row_spec(GROUPS), row_spec(GROUPS),
                   pl.BlockSpec((1, CARRY_ROWS, LANES), lambda i: (i, 0, 0))],
        out_shape=[jax.ShapeDtypeStruct((rows, D_MODEL), F32),
                   jax.ShapeDtypeStruct((rows, D_MODEL), F32),
                   jax.ShapeDtypeStruct((rows, N_HEADS), F32),
                   jax.ShapeDtypeStruct((rows, GROUPS), BF16),
                   jax.ShapeDtypeStruct((rows, GROUPS), BF16),
                   jax.ShapeDtypeStruct((rows, GROUPS), BF16),
                   jax.ShapeDtypeStruct((rows // tm, CARRY_ROWS, LANES), F32)],
        scratch_shapes=[pltpu.VMEM((CARRY_ROWS, LANES), F32)],
        compiler_params=_params(1),
        name="fox_proj",
    )(x, g, wqkv, wf, bf, pmat, gmat)


def _cache_prep_kernel(ck_ref, cv_ref, cl_ref, pmat_ref, ka_ref, va_ref, c_ref, *, past_len, tr):
    r = pl.program_id(1)

    @pl.when(r == 0)
    def _():
        c = _segment_cumsum(cl_ref[...], past_len)
        c_ref[...] = (c - c[past_len - 1:past_len, :]) * LOG2E

    c = c_ref[pl.ds(pl.multiple_of(r * tr, tr), tr), :]
    aug = _dot(_split3(c), pmat_ref[...])
    _store_groups(ck_ref[...], aug, ka_ref)
    _store_groups(cv_ref[...], None, va_ref)


def _cache_prep(ck, cv, cl, pmat_k, *, n_streams, past_len, tr):
    nr = past_len // tr
    row_spec = lambda w: pl.BlockSpec((tr, w), lambda n, r: (n * nr + r, 0))
    return pl.pallas_call(
        functools.partial(_cache_prep_kernel, past_len=past_len, tr=tr),
        grid=(n_streams, nr),
        in_specs=[row_spec(D_MODEL), row_spec(D_MODEL),
                  pl.BlockSpec((past_len, LANES), lambda n, r: (n, 0)),
                  _const_spec((LANES, GROUPS))],
        out_specs=[row_spec(GROUPS), row_spec(GROUPS)],
        out_shape=[jax.ShapeDtypeStruct((n_streams * past_len, GROUPS), BF16)] * 2,
        scratch_shapes=[pltpu.VMEM((past_len, LANES), F32)],
        compiler_params=_params(2),
        name="cache_prep",
    )(ck, cv, cl, pmat_k)


def _attend(q, k, v, m, acc, mask=None):
    s = lax.dot_general(q, k, (((1,), (1,)), ((), ())), preferred_element_type=F32)
    if mask is not None:
        s = jnp.where(mask, s, NEG)
    m_new = jnp.maximum(m, jnp.max(s, axis=-1, keepdims=True))
    p = jnp.exp2(s - m_new).astype(BF16)
    acc = jnp.exp2(m - m_new) * acc + _dot(p, v)
    return m_new, acc


def _causal_mask(n):
    return (lax.broadcasted_iota(jnp.int32, (n, n), 1)
            <= lax.broadcasted_iota(jnp.int32, (n, n), 0))


def _finish(accs):
    outs = [a[:, :HEAD_DIM] / a[:, HEAD_DIM:HEAD_DIM + 1] for a in accs]
    return jnp.concatenate(outs, axis=-1)


def _prompt_attn_kernel(trips_ref, qa_ref, ka_ref, va_ref, o_ref, *, tq):
    hp = pl.program_id(0)
    i = pl.program_id(1)
    nq = pl.num_programs(1)
    mask = _causal_mask(tq)
    sls = [slice(e * LANES, (e + 1) * LANES) for e in range(2)]
    qs = [qa_ref[:, sl] for sl in sls]
    n = [trips_ref[(2 * hp + e) * nq + i] for e in range(2)]

    def step(e, j, carry):
        half = tq // 2
        for part in range(2):
            rows = pl.ds(pl.multiple_of(j * tq + part * half, half), half)
            carry = _attend(qs[e], ka_ref[rows, sls[e]], va_ref[rows, sls[e]], *carry)
        return carry

    m0 = jnp.full((tq, 1), NEG, F32)
    acc0 = jnp.zeros((tq, LANES), F32)
    diag = pl.ds(pl.multiple_of(i * tq, tq), tq)
    carry = tuple(_attend(qs[e], ka_ref[diag, sls[e]], va_ref[diag, sls[e]], m0, acc0, mask)
                  for e in range(2))

    def both(jj, carry):
        return tuple(step(e, i - 1 - jj, carry[e]) for e in range(2))

    n_both = jnp.minimum(n[0], n[1])
    carry = lax.fori_loop(0, n_both, both, carry)
    accs = []
    for e in range(2):
        _, acc = lax.fori_loop(n_both, n[e], lambda jj, c, e=e: step(e, i - 1 - jj, c), carry[e])
        accs.append(acc)
    o_ref[...] = _finish(accs)


def _block_trips(stats, *, tiles_per_block):
    st = stats[:, :, :N_HEADS]
    nb = st.shape[0] // tiles_per_block
    per_block = lambda r: st[:, r].reshape(nb, tiles_per_block, N_HEADS)
    qn = per_block(0).max(axis=1)
    kn = per_block(1).max(axis=1)
    cmax = per_block(2).max(axis=1)
    kpre = per_block(3)[:, -1]
    cpre = per_block(4)[:, -1]
    bound = (NORM_SLACK * qn[:, None] * (kpre[None, :] + kn[:, None])
             + cmax[:, None] - cpre[None, :] + BOUND_SLACK)
    behind = jnp.arange(nb)[None, :] < jnp.arange(nb)[:, None]
    needed = (bound > -ZERO_WEIGHT_GAP) & behind[:, :, None]
    return needed.sum(axis=1).astype(jnp.int32).T.reshape(-1)


def _prompt_attn(trips, qa, ka, va, *, tq):
    rows = qa.shape[0]
    pair = 2 * LANES
    return pl.pallas_call(
        functools.partial(_prompt_attn_kernel, tq=tq),
        grid_spec=pltpu.PrefetchScalarGridSpec(
            num_scalar_prefetch=1,
            grid=(N_HEADS // 2, rows // tq),
            in_specs=[pl.BlockSpec((tq, pair), lambda hp, i, t: (i, hp)),
                      pl.BlockSpec((rows, pair), lambda hp, i, t: (0, hp)),
                      pl.BlockSpec((rows, pair), lambda hp, i, t: (0, hp))],
            out_specs=pl.BlockSpec((tq, LANES), lambda hp, i, t: (i, hp))),
        out_shape=jax.ShapeDtypeStruct((rows, D_MODEL), F32),
        compiler_params=_params(2),
        name="prompt_attn",
    )(trips, qa, ka, va)


def _sample_attn_kernel(qa_ref, kn_ref, vn_ref, kc_ref, vc_ref, o_ref):
    t = qa_ref.shape[0]
    mask = _causal_mask(t)
    accs = []
    for e in range(2):
        sl = slice(e * LANES, (e + 1) * LANES)
        q = qa_ref[:, sl]
        m0 = jnp.full((t, 1), NEG, F32)
        acc0 = jnp.zeros((t, LANES), F32)
        m, acc = _attend(q, kn_ref[:, sl], vn_ref[:, sl], m0, acc0, mask)
        _, acc = _attend(q, kc_ref[:, sl], vc_ref[:, sl], m, acc)
        accs.append(acc)
    o_ref[...] = _finish(accs)


def _sample_attn(qa, kn, vn, kc, vc, *, n_streams, seg, past_len):
    pair = 2 * LANES
    new_spec = pl.BlockSpec((seg, pair), lambda n, hp: (n, hp))
    old_spec = pl.BlockSpec((past_len, pair), lambda n, hp: (n, hp))
    return pl.pallas_call(
        _sample_attn_kernel,
        grid=(n_streams, N_HEADS // 2),
        in_specs=[new_spec, new_spec, new_spec, old_spec, old_spec],
        out_specs=pl.BlockSpec((seg, LANES), lambda n, hp: (n, hp)),
        out_shape=jax.ShapeDtypeStruct((n_streams * seg, D_MODEL), F32),
        compiler_params=_params(2),
        name="sample_attn",
    )(qa, kn, vn, kc, vc)


def _out_proj_kernel(x_ref, o_ref, w_ref, gm_ref, w1_ref, w2_ref, gf_ref, y_ref, *, final_norm):
    y = x_ref[...] + _dot(o_ref[...].astype(BF16), w_ref[...])
    y_ref[...] = _mlp_block(y, gm_ref, w1_ref, w2_ref, gf_ref, final_norm)


def _out_proj(x, o, w, mlp, *, tm, final_norm):
    rows = x.shape[0]
    spec = pl.BlockSpec((tm, D_MODEL), lambda i: (i, 0))
    return pl.pallas_call(
        functools.partial(_out_proj_kernel, final_norm=final_norm),
        grid=(rows // tm,),
        in_specs=[spec, spec, _const_spec((D_MODEL, D_MODEL))] + _mlp_specs(),
        out_specs=spec,
        out_shape=jax.ShapeDtypeStruct((rows, D_MODEL), F32),
        compiler_params=_params(1),
        name="out_proj",
    )(x, o, w, *mlp)


def _trunk(x, conv_past, cache, w, *, n_streams, seg, tm, tq):
    depth = w["mlp_w1"].shape[0]
    conv_states, ks, vs, lfs = [], [], [], []
    carried = n_streams == 1
    for i in range(depth):
        j = i // 2
        g_mix = w["norm_mix"][i:i + 1]
        mlp = (w["norm_mlp"][i:i + 1], w["mlp_w1"][i], w["mlp_w2"][i], w["norm_final"])
        last = i == depth - 1
        if i % 2 == 0:
            x, st = _conv_mixer(x, conv_past[j], g_mix, w["conv_w_in"][j], w["conv_w"][j],
                                w["conv_w_out"][j], mlp, n_streams=n_streams,
                                seg=tm if carried else seg, final_norm=last)
            conv_states.append(st)
        else:
            tp = min(tm, 256)
            k, v, lf, qa, ka, va, stats = _fox_proj(
                x, g_mix, w["fox_wqkv"][j], w["fox_wf"][j], w["fox_bf"][j], w["pmat"], w["gmat"],
                tm=tp, seg=tp if carried else seg, carried=carried)
            if cache is None:
                o = _prompt_attn(_block_trips(stats, tiles_per_block=tq // tp), qa, ka, va, tq=tq)
            else:
                ck, cv, cl = cache
                past_len = ck.shape[2]
                flat = n_streams * past_len
                kc, vc = _cache_prep(ck[j].reshape(flat, D_MODEL), cv[j].reshape(flat, D_MODEL),
                                     cl[j], w["pmat"][:, GROUPS:], n_streams=n_streams,
                                     past_len=past_len, tr=512)
                o = _sample_attn(qa, ka, va, kc, vc, n_streams=n_streams, seg=seg,
                                 past_len=past_len)
            x = _out_proj(x, o, w["fox_w_out"][j], mlp, tm=tm, final_norm=last)
            ks.append(k)
            vs.append(v)
            lfs.append(lf)
    return x, jnp.stack(conv_states), jnp.stack(ks), jnp.stack(vs), jnp.stack(lfs)


def kernel(x_prompt, x_sample, state_conv, cache_k, cache_v, cache_logf, norm_mix, norm_mlp,
           norm_final, conv_w_in, conv_w, conv_w_out, fox_w_in, fox_b_f, fox_w_out, mlp_w1, mlp_w2):
    d = D_MODEL
    n_fox = fox_w_in.shape[0]
    batch, seq, _ = x_prompt.shape
    dec_batch, dec_seq, _ = x_sample.shape
    past_len = cache_k.shape[2]
    assert batch == 1

    lane_pad = ((0, 0), (0, 0), (0, LANES - N_HEADS))
    w = dict(
        norm_mix=norm_mix, norm_mlp=norm_mlp, norm_final=norm_final.reshape(1, d),
        conv_w_in=conv_w_in.astype(BF16), conv_w=conv_w, conv_w_out=conv_w_out.astype(BF16),
        fox_wqkv=fox_w_in[:, :, :3 * d].astype(BF16),
        fox_wf=jnp.pad(fox_w_in[:, :, 3 * d:], lane_pad).astype(BF16),
        fox_bf=jnp.pad(fox_b_f.reshape(n_fox, 1, N_HEADS), lane_pad),
        fox_w_out=fox_w_out.astype(BF16),
        mlp_w1=mlp_w1.astype(BF16), mlp_w2=mlp_w2.astype(BF16),
        pmat=jnp.asarray(_placement_matrix(), BF16),
        gmat=jnp.asarray(np.repeat(np.eye(N_HEADS, LANES, dtype=np.float32), HEAD_DIM, axis=0), BF16),
    )

    zero_conv = jnp.zeros((conv_w.shape[0], batch, CONV_WIDTH - 1, d), F32)
    y_p, p_conv, p_k, p_v, p_lf = _trunk(
        x_prompt.reshape(seq, d), zero_conv, None, w, n_streams=1, seg=seq, tm=512, tq=512)

    cl = jnp.pad(cache_logf.reshape(n_fox, dec_batch * past_len, N_HEADS), lane_pad)
    y_s, s_conv, s_k, s_v, s_lf = _trunk(
        x_sample.reshape(dec_batch * dec_seq, d), state_conv, (cache_k, cache_v, cl), w,
        n_streams=dec_batch, seg=dec_seq, tm=dec_batch * dec_seq, tq=None)

    heads = (N_HEADS, HEAD_DIM)
    return (y_p.reshape(batch, seq, d), y_s.reshape(dec_batch, dec_seq, d),
            p_conv,
            p_k.reshape(n_fox, batch, seq, *heads), p_v.reshape(n_fox, batch, seq, *heads),
            p_lf.reshape(n_fox, batch, seq, N_HEADS),
            s_conv,
            s_k.reshape(n_fox, dec_batch, dec_seq, *heads),
            s_v.reshape(n_fox, dec_batch, dec_seq, *heads),
            s_lf.reshape(n_fox, dec_batch, dec_seq, N_HEADS))
```
